```python
import math
import jax, jax.numpy as jnp
from jax import lax
import numpy as np

D_MODEL = 2048
BATCH = 2
SEQ = 16384
DEPTH = 2

GRID_W = 64
CTX_LEN = 256
NORM_EPS = 1e-6
NEG_INF = -1e30

A_HEADS = 16
A_KV_HEADS = 4
A_HEAD_DIM = 64
WINDOW = 128
A_BLOCK = 128
ROPE_BASE = 10000.0

B_HEADS = 8
B_HEAD_DIM = 128
CONV_W = 5
DN_CHUNK = 64

C_HEADS = 4
C_KEY_DIM = D_MODEL // 2
C_VAL_DIM = D_MODEL
C_DK = C_KEY_DIM // C_HEADS
C_DV = C_VAL_DIM // C_HEADS
GATE_RANK = 16
GATE_NORMALIZER = 16.0
GLA_CHUNK = 64

N_EXPERTS = 32
TOP_K = 4
D_FF = D_MODEL
SWIGLU_ALPHA = 1.702
SWIGLU_LIMIT = 7.0
MOE_BLOCK = 512

A_Q = A_HEADS * A_HEAD_DIM
A_KV = A_KV_HEADS * A_HEAD_DIM
B_W = B_HEADS * B_HEAD_DIM
EVEN_IN = A_Q + 2 * A_KV + 4 * B_W + 4 * B_HEADS
EVEN_MIX = A_Q + B_W
ODD_IN = 2 * C_KEY_DIM + 2 * C_VAL_DIM + 2 * GATE_RANK
N_EVEN = (DEPTH + 1) // 2
N_ODD = DEPTH // 2

kernel_name = 'hybrid_swa_deltanet_gla_moe_dit'

f32 = jnp.float32


def rms_norm(x, g, eps=NORM_EPS):
    xf = x.astype(f32)
    y = xf * lax.rsqrt(jnp.mean(xf * xf, axis=-1, keepdims=True) + eps)
    return (y * g.astype(f32)).astype(x.dtype)


def l2norm(x, eps=1e-6):
    xf = x.astype(f32)
    return xf * lax.rsqrt(jnp.sum(xf * xf, axis=-1, keepdims=True) + eps)


def modulate(h, shift, scale):
    return h * (1.0 + scale) + shift


def split_cols(p, sizes):
    return jnp.split(p, np.cumsum(sizes)[:-1].tolist(), axis=-1)


def axial_rope_tables(rows):
    row = jnp.repeat(jnp.arange(rows, dtype=f32), GRID_W)
    col = jnp.tile(jnp.arange(GRID_W, dtype=f32), rows)
    n_freq = A_HEAD_DIM // 4
    inv = ROPE_BASE ** (-jnp.arange(n_freq, dtype=f32) / n_freq)
    ang = jnp.stack([row[:, None] * inv, col[:, None] * inv], axis=1)
    return jnp.cos(ang), jnp.sin(ang)


def apply_axial_rope(x, cos, sin):
    B, S, H, dh = x.shape
    xr = x.astype(f32).reshape(B, S, H, 2, 2, dh // 4)
    x1, x2 = xr[..., 0, :], xr[..., 1, :]
    c = cos[None, :, None]
    s = sin[None, :, None]
    out = jnp.stack([x1 * c - x2 * s, x2 * c + x1 * s], axis=-2)
    return out.reshape(B, S, H, dh).astype(x.dtype)


def centred_dwconv(x, w):
    half = CONV_W // 2
    return lax.conv_general_dilated(x, w[:, None, :].astype(x.dtype), window_strides=(1,),
                                    padding=[(half, half)], dimension_numbers=('NWC', 'WIO', 'NWC'),
                                    feature_group_count=x.shape[-1])


def window_attention(q, k, v, kc, vc, sinks):
    B, S, Hq, dh = q.shape
    Hkv = k.shape[2]
    G = Hq // Hkv
    nb = S // A_BLOCK
    L = kc.shape[1]
    nk = 3 * A_BLOCK
    scale = dh ** -0.5
    qb = q.reshape(B, nb, A_BLOCK, Hkv, G, dh)

    def band(t):
        tp = jnp.pad(t, ((0, 0), (A_BLOCK, A_BLOCK), (0, 0), (0, 0))).reshape(B, nb + 2, A_BLOCK, Hkv, dh)
        return jnp.concatenate([tp[:, :-2], tp[:, 1:-1], tp[:, 2:]], axis=2)

    kb, vb = band(k), band(v)
    s_loc = jnp.einsum('bnqhgd,bnkhd->bnhgqk', qb, kb).astype(f32) * scale
    q_pos = jnp.arange(nb)[:, None] * A_BLOCK + jnp.arange(A_BLOCK)[None, :]
    k_pos = (jnp.arange(nb)[:, None] - 1) * A_BLOCK + jnp.arange(nk)[None, :]
    kp = k_pos[:, None, :]
    allowed = (jnp.abs(kp - q_pos[:, :, None]) <= WINDOW) & (kp >= 0) & (kp < S)
    s_loc = jnp.where(allowed[None, :, None, None], s_loc, NEG_INF)
    s_ctx = jnp.einsum('bnqhgd,blhd->bnhgql', qb, kc).astype(f32) * scale
    sink = jnp.broadcast_to(sinks.reshape(Hkv, G)[None, None, :, :, None, None].astype(f32),
                            s_loc.shape[:-1] + (1,))
    p = jax.nn.softmax(jnp.concatenate([s_loc, s_ctx, sink], axis=-1), axis=-1)
    out = (jnp.einsum('bnhgqk,bnkhd->bnqhgd', p[..., :nk].astype(v.dtype), vb)
           + jnp.einsum('bnhgql,blhd->bnqhgd', p[..., nk:nk + L].astype(v.dtype), vc))
    return out.reshape(B, S, Hq * dh)


def context_attention(qc, kc, vc, sinks):
    B, L, Hq, dh = qc.shape
    Hkv = kc.shape[2]
    G = Hq // Hkv
    qg = qc.reshape(B, L, Hkv, G, dh)
    s = jnp.einsum('blhgd,bmhd->bhglm', qg, kc).astype(f32) * dh ** -0.5
    sink = jnp.broadcast_to(sinks.reshape(Hkv, G)[None, :, :, None, None].astype(f32), s.shape[:-1] + (1,))
    p = jax.nn.softmax(jnp.concatenate([s, sink], axis=-1), axis=-1)[..., :L]
    out = jnp.einsum('bhglm,bmhd->blhgd', p.astype(vc.dtype), vc)
    return out.reshape(B, L, Hq * dh)


def gated_delta_chunked(q, k, v, g, beta, s0):
    B, H, T, dk = k.shape
    dv = v.shape[-1]
    C = DN_CHUNK
    n = T // C
    q, k, v = [t.astype(f32).reshape(B, H, n, C, t.shape[-1]) for t in (q, k, v)]
    g = jnp.cumsum(g.astype(f32).reshape(B, H, n, C), axis=-1)
    beta = beta.astype(f32).reshape(B, H, n, C)
    tri_strict = jnp.tril(jnp.ones((C, C), bool), -1)
    tri_incl = jnp.tril(jnp.ones((C, C), bool))
    diff = g[..., :, None] - g[..., None, :]
    decay = jnp.where(tri_incl, jnp.exp(jnp.where(tri_incl, diff, 0.0)), 0.0)
    kb = k * beta[..., None]
    lmat = jnp.where(tri_strict, jnp.einsum('bhncd,bhnmd->bhncm', kb, k) * decay, 0.0)
    tmat = lmat + jnp.eye(C, dtype=f32)
    rhs = jnp.concatenate([v * beta[..., None], kb * jnp.exp(g)[..., None]], axis=-1)
    sol = lax.linalg.triangular_solve(tmat, rhs, left_side=True, lower=True, unit_diagonal=True)
    u, w = sol[..., :dv], sol[..., dv:]
    attn = jnp.where(tri_incl, jnp.einsum('bhncd,bhnmd->bhncm', q, k) * decay, 0.0)
    q_dec = q * jnp.exp(g)[..., None]
    k_dec = k * jnp.exp(g[..., -1:] - g)[..., None]
    g_last = jnp.exp(g[..., -1])

    def step(S, xs):
        u_i, w_i, a_i, qd_i, kd_i, gl_i = xs
        v_new = u_i - jnp.einsum('bhcd,bhde->bhce', w_i, S)
        o = jnp.einsum('bhcd,bhde->bhce', qd_i, S) + jnp.einsum('bhcm,bhme->bhce', a_i, v_new)
        S = S * gl_i[..., None, None] + jnp.einsum('bhcd,bhce->bhde', kd_i, v_new)
        return S, o

    xs = tuple(jnp.moveaxis(t, 2, 0) for t in (u, w, attn, q_dec, k_dec, g_last))
    s_fin, o = lax.scan(step, s0, xs)
    return jnp.moveaxis(o, 0, 2).reshape(B, H, T, dv), s_fin


def gla_chunked(q, k, v, gk, s0):
    B, H, T, dk = k.shape
    dv = v.shape[-1]
    C = GLA_CHUNK
    n = T // C
    q, k, v, gk = [t.astype(f32).reshape(B, H, n, C, t.shape[-1]) for t in (q, k, v, gk)]
    b = jnp.cumsum(gk, axis=3)
    b_mid = b[:, :, :, C // 2:C // 2 + 1]
    tri = jnp.tril(jnp.ones((C, C), bool))
    attn = jnp.where(tri, jnp.einsum('bhncd,bhnmd->bhncm', q * jnp.exp(b - b_mid), k * jnp.exp(b_mid - b)), 0.0)
    o_intra = jnp.einsum('bhncm,bhnme->bhnce', attn, v)
    b_last = b[:, :, :, -1:]
    q_dec = q * jnp.exp(b)
    k_dec = k * jnp.exp(b_last - b)
    dec = jnp.exp(b_last[:, :, :, 0])

    def step(S, xs):
        qd, kd, vv, dc = xs
        o = jnp.einsum('bhcd,bhde->bhce', qd, S)
        S = S * dc[..., None] + jnp.einsum('bhcd,bhce->bhde', kd, vv)
        return S, o

    xs = tuple(jnp.moveaxis(t, 2, 0) for t in (q_dec, k_dec, v, dec))
    s_fin, o_inter = lax.scan(step, s0, xs)
    o = jnp.moveaxis(o_inter, 0, 2) + o_intra
    return o.reshape(B, H, T, dv), s_fin


def bidirectional_prefix_scan(scan_fn, shared_lat, gates_lat, shared_ctx, gates_ctx, state_shape):
    B, H = shared_lat[0].shape[:2]
    s0 = jnp.zeros((B, H) + state_shape, f32)
    outs_l, outs_c = [], []
    for d in range(2):
        flip = (lambda t: jnp.flip(t, axis=2)) if d == 1 else (lambda t: t)
        o_c, s_c = scan_fn(*[flip(t) for t in shared_ctx + gates_ctx[d]], s0)
        o_l, _ = scan_fn(*[flip(t) for t in shared_lat + gates_lat[d]], s_c)
        outs_c.append(flip(o_c))
        outs_l.append(flip(o_l))
    return outs_l[0] + outs_l[1], outs_c[0] + outs_c[1]


def even_mixer(h_lat, h_ctx, w_in, w_out, q_gain, k_gain, sinks, conv_w, a_log, dt_bias, dn_norm, cos, sin, need_ctx):
    B = h_lat.shape[0]
    sizes = [A_Q, A_KV, A_KV, 3 * B_W, B_W, B_HEADS, B_HEADS, B_HEADS, B_HEADS]
    lat = split_cols(h_lat @ w_in, sizes)
    cx = split_cols(h_ctx @ w_in, sizes)

    def a_heads(qa, ka, va):
        T = qa.shape[1]
        return (rms_norm(qa.reshape(B, T, A_HEADS, A_HEAD_DIM), q_gain),
                rms_norm(ka.reshape(B, T, A_KV_HEADS, A_HEAD_DIM), k_gain),
                va.reshape(B, T, A_KV_HEADS, A_HEAD_DIM))

    q_l, k_l, v_l = a_heads(*lat[:3])
    q_l = apply_axial_rope(q_l, cos, sin)
    k_l = apply_axial_rope(k_l, cos, sin)
    q_c, k_c, v_c = a_heads(*cx[:3])
    a_lat = window_attention(q_l, k_l, v_l, k_c, v_c, sinks)

    def b_heads(qkv, a_f, a_b, b_f, b_b):
        T = qkv.shape[1]
        qkv = jax.nn.silu(centred_dwconv(qkv, conv_w))
        qb, kb, vb = [t.reshape(B, T, B_HEADS, B_HEAD_DIM).transpose(0, 2, 1, 3) for t in jnp.split(qkv, 3, axis=-1)]
        qb = l2norm(qb) * (B_HEAD_DIM ** -0.5)
        kb = l2norm(kb)
        gates = []
        for d, (a_d, b_d) in enumerate(((a_f, b_f), (a_b, b_b))):
            g = -jnp.exp(a_log[d].astype(f32)) * jax.nn.softplus(a_d.astype(f32) + dt_bias[d].astype(f32))
            beta = jax.nn.sigmoid(b_d.astype(f32))
            gates.append((g.transpose(0, 2, 1), beta.transpose(0, 2, 1)))
        return (qb, kb, vb), gates

    sh_l, gt_l = b_heads(lat[3], *lat[5:])
    sh_c, gt_c = b_heads(cx[3], *cx[5:])
    o_l, o_c = bidirectional_prefix_scan(gated_delta_chunked, sh_l, gt_l, sh_c, gt_c, (B_HEAD_DIM, B_HEAD_DIM))

    def b_out(o, z):
        T = z.shape[1]
        o = rms_norm(o.transpose(0, 2, 1, 3).astype(z.dtype), dn_norm)
        return (o * jax.nn.silu(z.reshape(B, T, B_HEADS, B_HEAD_DIM))).reshape(B, T, B_W)

    y_lat = jnp.concatenate([a_lat, b_out(o_l, lat[4])], axis=-1) @ w_out
    y_ctx = None
    if need_ctx:
        a_ctx = context_attention(q_c, k_c, v_c, sinks)
        y_ctx = jnp.concatenate([a_ctx, b_out(o_c, cx[4])], axis=-1) @ w_out
    return y_lat, y_ctx


def odd_mixer(h_lat, h_ctx, w_in, gate_w2, gate_b, gla_norm, w_out, need_ctx):
    B = h_lat.shape[0]
    sizes = [C_KEY_DIM, C_KEY_DIM, C_VAL_DIM, C_VAL_DIM, GATE_RANK, GATE_RANK]

    def heads(p):
        q, k, v, g_out, r_f, r_b = split_cols(p, sizes)
        T = q.shape[1]
        hd = lambda t, dd: t.reshape(B, T, C_HEADS, dd).transpose(0, 2, 1, 3)
        gates = []
        for d, r in enumerate((r_f, r_b)):
            gk = jax.nn.log_sigmoid((r @ gate_w2[d] + gate_b[d]).astype(f32)) / GATE_NORMALIZER
            gates.append((hd(gk, C_DK),))
        return (hd(q, C_DK) * (C_DK ** -0.5), hd(k, C_DK), hd(v, C_DV)), gates, g_out

    sh_l, gt_l, go_l = heads(h_lat @ w_in)
    sh_c, gt_c, go_c = heads(h_ctx @ w_in)
    o_l, o_c = bidirectional_prefix_scan(gla_chunked, sh_l, gt_l, sh_c, gt_c, (C_DK, C_DV))

    def out(o, g_out):
        T = g_out.shape[1]
        o = rms_norm(o.transpose(0, 2, 1, 3).astype(g_out.dtype), gla_norm)
        return (o * jax.nn.silu(g_out.reshape(B, T, C_HEADS, C_DV))).reshape(B, T, C_VAL_DIM) @ w_out

    return out(o_l, go_l), (out(o_c, go_c) if need_ctx else None)


def moe_ffn(h, w_router, b_router, w_gu, b_gu, w_down, b_down):
    N, D = h.shape
    logits = (h @ w_router + b_router).astype(f32)
    top_val, top_idx = lax.top_k(logits, TOP_K)
    gate = jax.nn.softmax(top_val, axis=-1)
    e_flat = top_idx.reshape(-1)
    tok_flat = jnp.repeat(jnp.arange(N, dtype=jnp.int32), TOP_K)
    order = jnp.argsort(e_flat)
    e_sorted = e_flat[order]
    counts = jnp.bincount(e_flat, length=N_EXPERTS)
    padded = (counts + MOE_BLOCK - 1) // MOE_BLOCK * MOE_BLOCK
    start = jnp.cumsum(counts) - counts
    pad_end = jnp.cumsum(padded)
    pad_start = pad_end - padded
    dest = pad_start[e_sorted] + jnp.arange(N * TOP_K) - start[e_sorted]
    n_blocks = -(-(N * TOP_K) // MOE_BLOCK) + N_EXPERTS
    P = n_blocks * MOE_BLOCK
    slot_tok = jnp.full((P,), N, jnp.int32).at[dest].set(tok_flat[order])
    slot_w = jnp.zeros((P,), h.dtype).at[dest].set(gate.reshape(-1)[order].astype(h.dtype))
    block_expert = jnp.minimum(jnp.searchsorted(pad_end, jnp.arange(n_blocks) * MOE_BLOCK, side='right'),
                               N_EXPERTS - 1)
    h_pad = jnp.concatenate([h, jnp.zeros((1, D), h.dtype)], axis=0)

    def expert_block(acc, blk):
        tok, wt, e = blk
        gu = h_pad[tok] @ w_gu[e] + b_gu[e]
        glu, lin = jnp.split(gu, 2, axis=-1)
        glu = jnp.minimum(glu, SWIGLU_LIMIT)
        lin = jnp.clip(lin, -SWIGLU_LIMIT, SWIGLU_LIMIT)
        act = glu * jax.nn.sigmoid(SWIGLU_ALPHA * glu) * (lin + 1.0)
        y = (act @ w_down[e] + b_down[e]) * wt[:, None]
        return acc.at[tok].add(y.astype(acc.dtype)), None

    out, _ = lax.scan(expert_block, jnp.zeros((N + 1, D), h.dtype),
                      (slot_tok.reshape(n_blocks, MOE_BLOCK), slot_w.reshape(n_blocks, MOE_BLOCK), block_expert))
    return out[:N]


def setup_inputs(seed: int = 0) -> dict:
    key = jax.random.key(seed)
    ks = iter(jax.random.split(key, 40))
    D = D_MODEL

    def nrm(shape, scale):
        return jax.random.normal(next(ks), shape, jnp.float32) * scale

    dt = jnp.exp(jax.random.uniform(next(ks), (N_EVEN, 2, B_HEADS), jnp.float32,
                                    minval=math.log(1e-3), maxval=math.log(1e-1)))
    a_log = jnp.log(jax.random.uniform(next(ks), (N_EVEN, 2, B_HEADS), jnp.float32, minval=1.0, maxval=16.0))
    return {
        'x': nrm((BATCH, SEQ, D), 1.0),
        'c': nrm((BATCH, D), 1.0),
        'ctx': nrm((BATCH, CTX_LEN, D), 1.0),
        'c_ctx': nrm((D,), 1.0),
        'w_mod': nrm((DEPTH, D, 6 * D), 0.5 * D ** -0.5),
        'b_mod': nrm((DEPTH, 6 * D), 0.02),
        'norm_g': 1.0 + nrm((DEPTH, 2, D), 0.05),
        'e_w_in': nrm((N_EVEN, D, EVEN_IN), D ** -0.5),
        'e_w_out': nrm((N_EVEN, EVEN_MIX, D), EVEN_MIX ** -0.5),
        'e_q_gain': 1.0 + nrm((N_EVEN, A_HEAD_DIM), 0.05),
        'e_k_gain': 1.0 + nrm((N_EVEN, A_HEAD_DIM), 0.05),
        'e_sinks': nrm((N_EVEN, A_HEADS), 0.5),
        'e_conv_w': nrm((N_EVEN, CONV_W, 3 * B_W), CONV_W ** -0.5),
        'e_a_log': a_log,
        'e_dt_bias': dt + jnp.log(-jnp.expm1(-dt)),
        'e_dn_norm': 1.0 + nrm((N_EVEN, B_HEAD_DIM), 0.05),
        'o_w_in': nrm((N_ODD, D, ODD_IN), D ** -0.5),
        'o_gate_w2': nrm((N_ODD, 2, GATE_RANK, C_KEY_DIM), GATE_RANK ** -0.5),
        'o_gate_b': nrm((N_ODD, 2, C_KEY_DIM), 0.1),
        'o_gla_norm': 1.0 + nrm((N_ODD, C_DV), 0.05),
        'o_w_out': nrm((N_ODD, C_VAL_DIM, D), C_VAL_DIM ** -0.5),
        'w_router': nrm((DEPTH, D, N_EXPERTS), D ** -0.5),
        'b_router': nrm((DEPTH, N_EXPERTS), 0.01),
        'w_gu': nrm((DEPTH, N_EXPERTS, D, 2 * D_FF), D ** -0.5),
        'b_gu': nrm((DEPTH, N_EXPERTS, 2 * D_FF), 0.02),
        'w_down': nrm((DEPTH, N_EXPERTS, D_FF, D), D_FF ** -0.5),
        'b_down': nrm((DEPTH, N_EXPERTS, D), 0.02),
    }


def reference(x, c, ctx, c_ctx, w_mod, b_mod, norm_g, e_w_in, e_w_out, e_q_gain, e_k_gain, e_sinks, e_conv_w,
              e_a_log, e_dt_bias, e_dn_norm, o_w_in, o_gate_w2, o_gate_b, o_gla_norm, o_w_out,
              w_router, b_router, w_gu, b_gu, w_down, b_down):
    B, S, D = x.shape
    rows = S // GRID_W
    cos, sin = axial_rope_tables(rows)
    sc = jax.nn.silu(c)
    scc = jax.nn.silu(c_ctx)
    for layer in range(DEPTH):
        need_ctx = layer < DEPTH - 1
        mod = jnp.split((sc @ w_mod[layer] + b_mod[layer])[:, None, :], 6, axis=-1)
        mod_c = jnp.split(scc @ w_mod[layer] + b_mod[layer], 6, axis=-1)
        h_lat = modulate(rms_norm(x, norm_g[layer, 0]), mod[0], mod[1])
        h_ctx = modulate(rms_norm(ctx, norm_g[layer, 0]), mod_c[0], mod_c[1])
        i = layer // 2
        if layer % 2 == 0:
            y_lat, y_ctx = even_mixer(h_lat, h_ctx, e_w_in[i], e_w_out[i], e_q_gain[i], e_k_gain[i], e_sinks[i],
                                      e_conv_w[i], e_a_log[i], e_dt_bias[i], e_dn_norm[i], cos, sin, need_ctx)
        else:
            y_lat, y_ctx = odd_mixer(h_lat, h_ctx, o_w_in[i], o_gate_w2[i], o_gate_b[i], o_gla_norm[i],
                                     o_w_out[i], need_ctx)
        x = x + mod[2] * y_lat
        h_lat = modulate(rms_norm(x, norm_g[layer, 1]), mod[3], mod[4])
        moe_w = (w_router[layer], b_router[layer], w_gu[layer], b_gu[layer], w_down[layer], b_down[layer])
        if need_ctx:
            ctx = ctx + mod_c[2] * y_ctx
            h_ctx = modulate(rms_norm(ctx, norm_g[layer, 1]), mod_c[3], mod_c[4])
            tokens = jnp.concatenate([h_lat.reshape(-1, D), h_ctx.reshape(-1, D)], axis=0)
            f = moe_ffn(tokens, *moe_w)
            x = x + mod[5] * f[:B * S].reshape(B, S, D)
            ctx = ctx + mod_c[5] * f[B * S:].reshape(ctx.shape)
        else:
            x = x + mod[5] * moe_ffn(h_lat.reshape(-1, D), *moe_w).reshape(B, S, D)
    return x
```

```python
import functools

import numpy as np
import jax
import jax.numpy as jnp
from jax import lax
from jax.experimental import pallas as pl
from jax.experimental.pallas import tpu as pltpu

f32 = jnp.float32
bf16 = jnp.bfloat16

NORM_EPS = 1e-6
NEG_INF = -1e30
GRID_W = 64
ROPE_BASE = 10000.0
A_HEADS, A_KV_HEADS, A_HEAD_DIM, WINDOW = 16, 4, 64, 128
A_GROUP = A_HEADS // A_KV_HEADS
B_HEADS, B_HEAD_DIM, CONV_W = 8, 128, 5
CHUNK = 64
C_HEADS = 4
GATE_RANK = 16
GATE_NORMALIZER = 16.0
TOP_K = 4
SWIGLU_ALPHA, SWIGLU_LIMIT = 1.702, 7.0

LANES = 128
SEQ_BLOCK = 256
VMEM_LIMIT = 56 * 1024 * 1024
MOE_ROWS = 512
MOE_FF_TILE = 512


def _cparams(*sem):
    return pltpu.CompilerParams(dimension_semantics=sem, vmem_limit_bytes=VMEM_LIMIT)


def _split2(x):
    hi = x.astype(bf16)
    return hi, (x - hi.astype(f32)).astype(bf16)


def _split3(x):
    x1 = x.astype(bf16)
    r1 = x - x1.astype(f32)
    x2 = r1.astype(bf16)
    x3 = (r1 - x2.astype(f32)).astype(bf16)
    return x1, x2, x3


def _mm(a, b):
    return jnp.dot(a, b, preferred_element_type=f32)


def _mm_nt(a, b):
    return lax.dot_general(a, b, (((1,), (1,)), ((), ())), preferred_element_type=f32)


def _mm_lhs01(a01, x):
    x1, x2, x3 = _split3(x)
    return _mm(a01, x1) + _mm(a01, x2) + _mm(a01, x3)


def _mm_rhs01(x, b01):
    x1, x2, x3 = _split3(x)
    return _mm(x1, b01) + _mm(x2, b01) + _mm(x3, b01)


def _mm_x3(a, b):
    a1, a2 = _split2(a)
    b1, b2 = _split2(b)
    return _mm(a1, b1) + _mm(a1, b2) + _mm(a2, b1)


def _sigmoid(x):
    return 1.0 / (1.0 + jnp.exp(-x))


def _softplus(x):
    return jnp.maximum(x, 0.0) + jnp.log(1.0 + jnp.exp(-jnp.abs(x)))


def _mask01(m):
    return jnp.where(m, 1.0, 0.0).astype(bf16)


def _chunk_masks(n, rev):
    row = lax.broadcasted_iota(jnp.int32, (n, n), 0)
    col = lax.broadcasted_iota(jnp.int32, (n, n), 1)
    same = (row // CHUNK) == (col // CHUNK)
    if rev:
        return same, same & (col >= row), same & (col > row), row, col
    return same, same & (col <= row), same & (col < row), row, col


def _mod_kernel(c_ref, w_ref, b_ref, o_ref):
    c = c_ref[...]
    o_ref[0] = _mm_x3(c * _sigmoid(c), w_ref[0]) + b_ref[0]


def _mod_vectors(c8, w_mod, b_mod):
    depth, d, n = w_mod.shape
    tn = 1024
    return pl.pallas_call(
        _mod_kernel,
        grid=(depth, n // tn),
        in_specs=[pl.BlockSpec((8, d), lambda l, j: (0, 0)),
                  pl.BlockSpec((1, d, tn), lambda l, j: (l, 0, j)),
                  pl.BlockSpec((1, 1, tn), lambda l, j: (l, 0, j))],
        out_specs=pl.BlockSpec((1, 8, tn), lambda l, j: (l, 0, j)),
        out_shape=jax.ShapeDtypeStruct((depth, 8, n), f32),
        compiler_params=_cparams("parallel", "parallel"),
        name="mod_vectors",
    )(c8, w_mod, b_mod.reshape(depth, 1, n))


def _proj_kernel(x_ref, g_ref, mb_ref, mc_ref, w_ref, o_ref, h_ref, *, tm, n_lat, sub):
    i = pl.program_id(1)

    @pl.when(pl.program_id(2) == 0)
    def _():
        for r0 in range(0, tm, sub):
            x = x_ref[0, r0:r0 + sub, :]
            y = x * lax.rsqrt(jnp.mean(x * x, axis=-1, keepdims=True) + NORM_EPS) * g_ref[...]
            row = i * tm + r0 + lax.broadcasted_iota(jnp.int32, (sub, 1), 0)
            is_ctx = row >= n_lat
            shift = jnp.where(is_ctx, mc_ref[0, 0:1, :], mb_ref[0, 0:1, :])
            scale = jnp.where(is_ctx, mc_ref[0, 1:2, :], mb_ref[0, 1:2, :])
            h_ref[r0:r0 + sub, :] = (y * (1.0 + scale) + shift).astype(bf16)

    o_ref[0] = _mm(h_ref[...], w_ref[...])


def _input_projection(x_all, gain, mod_l, w, n_lat, tn):
    b, t_all, d = x_all.shape
    n = w.shape[1]
    tm = 640 if t_all % 640 == 0 else SEQ_BLOCK
    assert t_all % tm == 0 and n % tn == 0
    return pl.pallas_call(
        functools.partial(_proj_kernel, tm=tm, n_lat=n_lat, sub=128),
        grid=(b, t_all // tm, n // tn),
        in_specs=[pl.BlockSpec((1, tm, d), lambda bi, i, j: (bi, i, 0)),
                  pl.BlockSpec((1, d), lambda bi, i, j: (0, 0)),
                  pl.BlockSpec((1, 6, d), lambda bi, i, j: (bi, 0, 0)),
                  pl.BlockSpec((1, 6, d), lambda bi, i, j: (2, 0, 0)),
                  pl.BlockSpec((d, tn), lambda bi, i, j: (0, j))],
        out_specs=pl.BlockSpec((1, tm, tn), lambda bi, i, j: (bi, i, j)),
        out_shape=jax.ShapeDtypeStruct((b, t_all, n), f32),
        scratch_shapes=[pltpu.VMEM((tm, d), bf16)],
        compiler_params=_cparams("parallel", "parallel", "arbitrary"),
        name="input_projection",
    )(x_all, gain.reshape(1, d), mod_l, mod_l, w)


def _qkprep_kernel(q_ref, kv_ref, cos_ref, sin_ref, gain_ref, ones_ref, e_ref, qo_ref, ko_ref, vo_ref):
    cos = cos_ref[...]
    sin = sin_ref[...]
    lane = lax.broadcasted_iota(jnp.int32, cos.shape, 1)
    upper = (lane & 32) != 0
    ones = ones_ref[...]

    def norm_rope(x, gain):
        ms = _mm_rhs01(x * x, ones) * (1.0 / A_HEAD_DIM)
        y = x * lax.rsqrt(ms + NORM_EPS) * gain
        outs = []
        for c in range(2):
            yc = y[:, c * LANES:(c + 1) * LANES]
            partner = jnp.where(upper, pltpu.roll(yc, 32, 1), pltpu.roll(yc, 96, 1))
            outs.append(yc * cos + partner * sin)
        return jnp.concatenate(outs, axis=1)

    for c in range(4):
        sl = slice(c * 256, (c + 1) * 256)
        qo_ref[0, :, sl] = norm_rope(q_ref[0, :, sl], gain_ref[:, sl]).astype(bf16)
    k = norm_rope(kv_ref[0, :, 0:256], gain_ref[:, 1024:1280]).astype(bf16)
    ko_ref[0] = _mm(k, e_ref[...]).astype(bf16)
    vo_ref[0] = _mm(kv_ref[0, :, 256:512].astype(bf16), e_ref[...]).astype(bf16)


def _qk_prepare(proj, cos_t, sin_t, gain, kv_block):
    b, t_all, _ = proj.shape
    tt = SEQ_BLOCK
    hd = A_HEAD_DIM
    ones = np.kron(np.eye(4, dtype=np.float32), np.ones((hd, hd), np.float32))
    expand = np.kron(np.eye(4, dtype=np.float32), np.tile(np.eye(hd, dtype=np.float32), (1, A_GROUP)))
    out = jax.ShapeDtypeStruct((b, t_all, 1024), bf16)
    return pl.pallas_call(
        _qkprep_kernel,
        grid=(b, t_all // tt),
        in_specs=[pl.BlockSpec((1, tt, 1024), lambda bi, i: (bi, i, 0)),
                  pl.BlockSpec((1, tt, 512), lambda bi, i: (bi, i, kv_block)),
                  pl.BlockSpec((tt, LANES), lambda bi, i: (i, 0)),
                  pl.BlockSpec((tt, LANES), lambda bi, i: (i, 0)),
                  pl.BlockSpec((1, 1280), lambda bi, i: (0, 0)),
                  pl.BlockSpec((256, 256), lambda bi, i: (0, 0)),
                  pl.BlockSpec((256, 1024), lambda bi, i: (0, 0))],
        out_specs=[pl.BlockSpec((1, tt, 1024), lambda bi, i: (bi, i, 0))] * 3,
        out_shape=[out, out, out],
        compiler_params=_cparams("parallel", "parallel"),
        name="qk_prepare",
    )(proj, proj, cos_t, sin_t, gain, jnp.asarray(ones, bf16), jnp.asarray(expand, bf16))


def _attn_kernel(sink_ref, q_ref, kp_ref, kc_ref, kn_ref, kx_ref, vp_ref, vc_ref, vn_ref, vx_ref, o_ref, *,
                 nq_lat, n_ctx):
    n = pl.program_id(1)
    blk = WINDOW
    rows = A_GROUP * blk
    nband = 3 * blk
    ncol = nband + n_ctx
    rq = lax.broadcasted_iota(jnp.int32, (rows, ncol), 0) % blk
    col = lax.broadcasted_iota(jnp.int32, (rows, ncol), 1)
    rgrp = lax.broadcasted_iota(jnp.int32, (rows, 1), 0) // blk
    dist = col - rq
    is_lat = n < nq_lat
    lo = jnp.where(n > 0, 0, blk)
    hi = jnp.where(is_lat, jnp.where(n < nq_lat - 1, nband, 2 * blk), 0)
    allowed = (col >= nband) | ((dist >= 0) & (dist <= 2 * WINDOW) & (col >= lo) & (col < hi))
    lane_g = lax.broadcasted_iota(jnp.int32, (blk, A_GROUP * A_HEAD_DIM), 1) // A_HEAD_DIM
    for h in range(A_KV_HEADS):
        sl = slice(h * 256, (h + 1) * 256)
        qh = q_ref[0, :, sl]
        qs = jnp.concatenate([jnp.where(lane_g == g, qh, jnp.zeros_like(qh)) for g in range(A_GROUP)], axis=0)
        kcat = jnp.concatenate([kp_ref[0, :, sl], kc_ref[0, :, sl], kn_ref[0, :, sl], kx_ref[0, :, sl]], axis=0)
        vcat = jnp.concatenate([vp_ref[0, :, sl], vc_ref[0, :, sl], vn_ref[0, :, sl], vx_ref[0, :, sl]], axis=0)
        s = jnp.where(allowed, _mm_nt(qs, kcat), NEG_INF)
        sink = jnp.zeros((rows, 1), f32)
        for g in range(A_GROUP):
            sink = jnp.where(rgrp == g, sink_ref[h * A_GROUP + g], sink)
        m = jnp.maximum(jnp.max(s, axis=-1, keepdims=True), sink)
        p = jnp.exp(s - m)
        den = jnp.sum(p, axis=-1, keepdims=True) + jnp.exp(sink - m)
        o = _mm(p.astype(bf16), vcat) * (1.0 / den)
        out = jnp.zeros((blk, 256), f32)
        for g in range(A_GROUP):
            out = out + jnp.where(lane_g == g, o[g * blk:(g + 1) * blk, :], 0.0)
        o_ref[0, :, sl] = out.astype(bf16)


def _attention(sinks, q, kt, vt, n_lat):
    b, t_all, w = q.shape
    blk = WINDOW
    nq = t_all // blk
    n_ctx = t_all - n_lat
    ctx_blk = n_lat // n_ctx
    qspec = pl.BlockSpec((1, blk, w), lambda bi, i: (bi, i, 0))
    prev = pl.BlockSpec((1, blk, w), lambda bi, i: (bi, jnp.maximum(i - 1, 0), 0))
    nxt = pl.BlockSpec((1, blk, w), lambda bi, i: (bi, jnp.minimum(i + 1, nq - 1), 0))
    ctx = pl.BlockSpec((1, n_ctx, w), lambda bi, i: (bi, ctx_blk, 0))
    return pl.pallas_call(
        functools.partial(_attn_kernel, nq_lat=n_lat // blk, n_ctx=n_ctx),
        grid=(b, nq),
        in_specs=[pl.BlockSpec(memory_space=pltpu.SMEM), qspec, prev, qspec, nxt, ctx, prev, qspec, nxt, ctx],
        out_specs=qspec,
        out_shape=jax.ShapeDtypeStruct((b, t_all, w), bf16),
        compiler_params=_cparams("parallel", "parallel"),
        name="window_attention",
    )(sinks, q, kt, kt, kt, kt, vt, vt, vt, vt)


def _dnprep_kernel(x_ref, xp_ref, xn_ref, w_ref, o_ref, ext_ref, *, nt_lat, tt):
    i = pl.program_id(1)
    j = pl.program_id(2)
    keep_prev = jnp.where((i == 0) | (i == nt_lat), 0.0, 1.0)
    keep_next = jnp.where((i == nt_lat - 1) | (i == nt_lat), 0.0, 1.0)
    ext_ref[0:8, :] = xp_ref[0] * keep_prev
    ext_ref[8:8 + tt, :] = x_ref[0]
    ext_ref[8 + tt:16 + tt, :] = xn_ref[0] * keep_next
    half = CONV_W // 2
    acc = jnp.zeros((tt, x_ref.shape[2]), f32)
    for tap in range(CONV_W):
        acc = acc + ext_ref[pl.ds(8 - half + tap, tt), :] * w_ref[tap:tap + 1, :]
    y = acc * _sigmoid(acc)
    q_scale = jnp.where(j == 0, B_HEAD_DIM ** -0.5, 1.0)
    for h in range(x_ref.shape[2] // B_HEAD_DIM):
        sl = slice(h * B_HEAD_DIM, (h + 1) * B_HEAD_DIM)
        yh = y[:, sl]
        rn = lax.rsqrt(jnp.sum(yh * yh, axis=-1, keepdims=True) + 1e-6) * q_scale
        o_ref[0, :, sl] = yh * jnp.where(j == 2, 1.0, rn)


def _dn_prepare(proj, conv_w8, n_lat, col_block):
    b, t_all, _ = proj.shape
    tt = SEQ_BLOCK
    bw = B_HEADS * B_HEAD_DIM
    r8 = tt // 8
    return pl.pallas_call(
        functools.partial(_dnprep_kernel, nt_lat=n_lat // tt, tt=tt),
        grid=(b, t_all // tt, 3),
        in_specs=[pl.BlockSpec((1, tt, bw), lambda bi, i, j: (bi, i, col_block + j)),
                  pl.BlockSpec((1, 8, bw), lambda bi, i, j: (bi, jnp.maximum(i * r8 - 1, 0), col_block + j)),
                  pl.BlockSpec((1, 8, bw), lambda bi, i, j: (bi, jnp.minimum((i + 1) * r8, t_all // 8 - 1),
                                                             col_block + j)),
                  pl.BlockSpec((8, bw), lambda bi, i, j: (0, j))],
        out_specs=pl.BlockSpec((1, tt, bw), lambda bi, i, j: (bi, i, j)),
        out_shape=jax.ShapeDtypeStruct((b, t_all, 3 * bw), f32),
        scratch_shapes=[pltpu.VMEM((tt + 16, bw), f32)],
        compiler_params=_cparams("parallel", "parallel", "parallel"),
        name="deltanet_prepare",
    )(proj, proj, proj, conv_w8)


def _dnscan_kernel(*refs, rev, add_prev):
    if add_prev:
        (q_ref, k_ref, v_ref, gc_ref, gt_ref, alr_ref, dtr_ref, alc_ref, dtc_ref, op_ref, o_ref,
         s_ref, u_scr, w_scr, qd_scr, kdt_scr, at_scr, vn_scr, egl_scr) = refs
    else:
        (q_ref, k_ref, v_ref, gc_ref, gt_ref, alr_ref, dtr_ref, alc_ref, dtc_ref, o_ref,
         s_ref, u_scr, w_scr, qd_scr, kdt_scr, at_scr, vn_scr, egl_scr) = refs
        op_ref = None
    tt = q_ref.shape[1]
    hd = B_HEAD_DIM
    d = 1 if rev else 0

    @pl.when(pl.program_id(1) == 0)
    def _():
        s_ref[...] = jnp.zeros(s_ref.shape, f32)

    same, incl, strict, row, col = _chunk_masks(tt, rev)
    tri = _mask01(incl)
    tri_t = _mask01(same & ((row >= col) if rev else (row <= col)))
    ones_bd = _mask01(same)
    eye = jnp.where(row == col, 1.0, 0.0)

    gates = gc_ref[0]
    g_all = -jnp.exp(alr_ref[...]) * _softplus(gates + dtr_ref[...])
    beta_all = _sigmoid(gates)
    gcum = _mm_lhs01(tri, g_all)
    gsum = _mm_lhs01(ones_bd, g_all)
    egl_scr[...] = jnp.exp(gsum)
    g_row = -jnp.exp(alc_ref[...]) * _softplus(gt_ref[0] + dtc_ref[...])
    gcum_r = _mm_rhs01(g_row, tri_t)

    for h in range(B_HEADS):
        li = d * B_HEADS + h
        sl = slice(h * hd, (h + 1) * hd)
        qh, kh, vh = q_ref[0, :, sl], k_ref[0, :, sl], v_ref[0, :, sl]
        gc = gcum[:, li:li + 1]
        gs = gsum[:, li:li + 1]
        beta = beta_all[:, 2 * B_HEADS + li:2 * B_HEADS + li + 1]
        dec = jnp.where(incl, jnp.exp(jnp.where(incl, gc - gcum_r[li:li + 1, :], 0.0)), 0.0)
        kb = kh * beta
        kh16 = kh.astype(bf16)
        m = -jnp.where(strict, _mm_nt(kb.astype(bf16), kh16) * dec, 0.0)
        p = eye + m
        for _ in range(5):
            m16 = m.astype(bf16)
            m = _mm(m16, m16)
            p = p + _mm(p.astype(bf16), m.astype(bf16))
        egc = jnp.exp(gc)
        rhs = jnp.concatenate([vh * beta, kb * egc], axis=1)
        sol = _mm(p.astype(bf16), rhs.astype(bf16))
        u_scr[h] = sol[:, :hd]
        vn_scr[h] = sol[:, :hd]
        w_scr[h] = sol[:, hd:]
        at_scr[h] = jnp.where(incl, _mm_nt(qh.astype(bf16), kh16) * dec, 0.0)
        qd_scr[h] = qh * egc
        kdt_scr[h] = (kh * jnp.exp(gs - gc)).T

    colk = lax.broadcasted_iota(jnp.int32, (hd, tt), 1) // CHUNK
    nchunk = tt // CHUNK
    for c in (range(nchunk - 1, -1, -1) if rev else range(nchunk)):
        rs = slice(c * CHUNK, (c + 1) * CHUNK)
        egl_row = egl_scr[c * CHUNK:c * CHUNK + 1, :]
        for h in range(B_HEADS):
            li = d * B_HEADS + h
            sl = slice(h * hd, (h + 1) * hd)
            s = s_ref[h]
            s16 = s.astype(bf16)
            vnew = u_scr[h, rs, :] - _mm(w_scr[h, rs, :].astype(bf16), s16)
            vn_scr[h, rs, :] = vnew
            vn16 = vn_scr[h].astype(bf16)
            o = _mm(qd_scr[h, rs, :].astype(bf16), s16) + _mm(at_scr[h, rs, :].astype(bf16), vn16)
            if op_ref is not None:
                o = o + op_ref[0, rs, sl]
            o_ref[0, rs, sl] = o
            kdt = jnp.where(colk == c, kdt_scr[h], 0.0).astype(bf16)
            s_ref[h] = s * egl_row[:, li:li + 1] + _mm(kdt, vn16)


def _seq_block_index(i, nt_lat, rev):
    lat = (nt_lat - i) if rev else (i - 1)
    return jnp.where(i == 0, nt_lat, lat)


def _dn_scan(qkv, proj, gates_t, alog, dtb, o_prev, n_lat, gate_block, rev):
    b, t_all, _ = qkv.shape
    tt = SEQ_BLOCK
    nt_lat = n_lat // tt
    bw = B_HEADS * B_HEAD_DIM
    hd = B_HEAD_DIM
    blk = lambda bi, i: _seq_block_index(i, nt_lat, rev)
    pad = jnp.zeros((LANES - 2 * B_HEADS,), f32)
    alr = jnp.concatenate([alog.reshape(-1), pad]).reshape(1, LANES)
    dtr = jnp.concatenate([dtb.reshape(-1), pad]).reshape(1, LANES)
    alc = jnp.concatenate([alog.reshape(-1), pad[:2 * B_HEADS]]).reshape(4 * B_HEADS, 1)
    dtc = jnp.concatenate([dtb.reshape(-1), pad[:2 * B_HEADS]]).reshape(4 * B_HEADS, 1)
    small = lambda shape: pl.BlockSpec(shape, lambda bi, i: (0, 0))
    in_specs = [pl.BlockSpec((1, tt, bw), lambda bi, i: (bi, blk(bi, i), 0)),
                pl.BlockSpec((1, tt, bw), lambda bi, i: (bi, blk(bi, i), 1)),
                pl.BlockSpec((1, tt, bw), lambda bi, i: (bi, blk(bi, i), 2)),
                pl.BlockSpec((1, tt, LANES), lambda bi, i: (bi, blk(bi, i), gate_block)),
                pl.BlockSpec((1, 4 * B_HEADS, tt), lambda bi, i: (bi, 0, blk(bi, i))),
                small((1, LANES)), small((1, LANES)), small((4 * B_HEADS, 1)), small((4 * B_HEADS, 1))]
    args = [qkv, qkv, qkv, proj, gates_t, alr, dtr, alc, dtc]
    if o_prev is not None:
        in_specs.append(pl.BlockSpec((1, tt, bw), lambda bi, i: (bi, blk(bi, i), 0)))
        args.append(o_prev)
    head = lambda *s: pltpu.VMEM((B_HEADS,) + s, f32)
    return pl.pallas_call(
        functools.partial(_dnscan_kernel, rev=rev, add_prev=o_prev is not None),
        grid=(b, nt_lat + 1),
        in_specs=in_specs,
        out_specs=pl.BlockSpec((1, tt, bw), lambda bi, i: (bi, blk(bi, i), 0)),
        out_shape=jax.ShapeDtypeStruct((b, t_all, bw), f32),
        scratch_shapes=[head(hd, hd), head(tt, hd), head(tt, hd), head(tt, hd), head(hd, tt), head(tt, tt),
                        head(tt, hd), pltpu.VMEM((tt, LANES), f32)],
        compiler_params=_cparams("parallel", "arbitrary"),
        name="deltanet_scan_bwd" if rev else "deltanet_scan_fwd",
    )(*args)


def _glascan_kernel(*refs, rev, add_prev):
    if add_prev:
        q_ref, k_ref, v_ref, r_ref, w2_ref, gb_ref, op_ref, o_ref, s_ref = refs
    else:
        q_ref, k_ref, v_ref, r_ref, w2_ref, gb_ref, o_ref, s_ref = refs
        op_ref = None
    tt, dk = q_ref.shape[1], q_ref.shape[2]

    @pl.when(pl.program_id(2) == 0)
    def _():
        s_ref[...] = jnp.zeros(s_ref.shape, f32)

    same, incl, _, _, _ = _chunk_masks(tt, rev)
    tri = _mask01(incl)
    ones_bd = _mask01(same)
    z = _mm_x3(r_ref[0], w2_ref[...]) + gb_ref[...]
    gk = (jnp.minimum(z, 0.0) - jnp.log(1.0 + jnp.exp(-jnp.abs(z)))) * (1.0 / GATE_NORMALIZER)
    bcum = _mm_lhs01(tri, gk)
    bsum = _mm_lhs01(ones_bd, gk)
    nchunk = tt // CHUNK
    mid = (CHUNK - 1 - CHUNK // 2) if rev else CHUNK // 2
    bmid = jnp.concatenate(
        [jnp.broadcast_to(bcum[c * CHUNK + mid:c * CHUNK + mid + 1, :], (CHUNK, dk)) for c in range(nchunk)], axis=0)
    q = q_ref[0] * (dk ** -0.5)
    k = k_ref[0]
    v16 = v_ref[0].astype(bf16)
    attn = jnp.where(incl, _mm_nt((q * jnp.exp(bcum - bmid)).astype(bf16),
                                  (k * jnp.exp(bmid - bcum)).astype(bf16)), 0.0)
    o_intra = _mm(attn.astype(bf16), v16)
    q_dec = (q * jnp.exp(bcum)).astype(bf16)
    kd_t = (k * jnp.exp(bsum - bcum)).T
    dec_t = jnp.exp(bsum).T
    colk = lax.broadcasted_iota(jnp.int32, (dk, tt), 1) // CHUNK
    for c in (range(nchunk - 1, -1, -1) if rev else range(nchunk)):
        rs = slice(c * CHUNK, (c + 1) * CHUNK)
        s = s_ref[...]
        o = o_intra[rs, :] + _mm(q_dec[rs, :], s.astype(bf16))
        if op_ref is not None:
            o = o + op_ref[0, rs, :]
        o_ref[0, rs, :] = o
        kdt = jnp.where(colk == c, kd_t, 0.0).astype(bf16)
        s_ref[...] = s * dec_t[:, c * CHUNK:c * CHUNK + 1] + _mm(kdt, v16)


def _gla_scan(proj, w2pad, gate_b, o_prev, n_lat, rev):
    b, t_all, _ = proj.shape
    tt = SEQ_BLOCK
    nt_lat = n_lat // tt
    dk = w2pad.shape[1] // C_HEADS
    dv = 2 * dk
    nq = C_HEADS * dk // dk
    blk = lambda i: _seq_block_index(i, nt_lat, rev)
    in_specs = [pl.BlockSpec((1, tt, dk), lambda bi, h, i: (bi, blk(i), h)),
                pl.BlockSpec((1, tt, dk), lambda bi, h, i: (bi, blk(i), nq + h)),
                pl.BlockSpec((1, tt, dv), lambda bi, h, i: (bi, blk(i), nq + h)),
                pl.BlockSpec((1, tt, LANES), lambda bi, h, i: (bi, blk(i), 6 * C_HEADS * dk // LANES)),
                pl.BlockSpec((LANES, dk), lambda bi, h, i: (0, h)),
                pl.BlockSpec((1, dk), lambda bi, h, i: (0, h))]
    args = [proj, proj, proj, proj, w2pad, gate_b]
    if o_prev is not None:
        in_specs.append(pl.BlockSpec((1, tt, dv), lambda bi, h, i: (bi, blk(i), h)))
        args.append(o_prev)
    return pl.pallas_call(
        functools.partial(_glascan_kernel, rev=rev, add_prev=o_prev is not None),
        grid=(b, C_HEADS, nt_lat + 1),
        in_specs=in_specs,
        out_specs=pl.BlockSpec((1, tt, dv), lambda bi, h, i: (bi, blk(i), h)),
        out_shape=jax.ShapeDtypeStruct((b, t_all, C_HEADS * dv), f32),
        scratch_shapes=[pltpu.VMEM((dk, dv), f32)],
        compiler_params=_cparams("parallel", "parallel", "arbitrary"),
        name="gla_scan_bwd" if rev else "gla_scan_fwd",
    )(*args)


def _out_kernel(*refs, group, has_a):
    if has_a:
        a_ref, o_ref, z_ref, gn_ref, w_ref, x_ref, mod_ref, g2_ref, wr_ref, br_ref, xo_ref, ho_ref, lo_ref = refs
    else:
        o_ref, z_ref, gn_ref, w_ref, x_ref, mod_ref, g2_ref, wr_ref, br_ref, xo_ref, ho_ref, lo_ref = refs
    width = o_ref.shape[2]
    z = z_ref[0]
    gate = z * _sigmoid(z)
    parts = []
    for g0 in range(0, width, group):
        og = o_ref[0, :, g0:g0 + group]
        parts.append(og * lax.rsqrt(jnp.mean(og * og, axis=-1, keepdims=True) + NORM_EPS))
    on = (jnp.concatenate(parts, axis=1) * gn_ref[...] * gate).astype(bf16)
    if has_a:
        na = a_ref.shape[2]
        y = _mm(a_ref[0], w_ref[0:na, :]) + _mm(on, w_ref[na:na + width, :])
    else:
        y = _mm(on, w_ref[...])
    xn = x_ref[0] + mod_ref[0, 2:3, :] * y
    xo_ref[0] = xn
    hn = xn * lax.rsqrt(jnp.mean(xn * xn, axis=-1, keepdims=True) + NORM_EPS) * g2_ref[...]
    h2 = hn * (1.0 + mod_ref[0, 4:5, :]) + mod_ref[0, 3:4, :]
    ho_ref[0] = h2
    lo_ref[0] = _mm_x3(h2, wr_ref[...]) + br_ref[...]


def _mixer_output(a, o, proj, z_block, gnorm, w_out, x_all, mod_l, g2, wr, br, n_lat, group):
    b, t_all, d = x_all.shape
    tm = SEQ_BLOCK
    nt_lat = n_lat // tm
    width = o.shape[2]
    row = lambda w: pl.BlockSpec((1, tm, w), lambda bi, i: (bi, i, 0))
    const = lambda shape: pl.BlockSpec(shape, lambda bi, i: (0,) * len(shape))
    in_specs, args = [], []
    if a is not None:
        in_specs.append(row(a.shape[2]))
        args.append(a)
    in_specs += [row(width),
                 pl.BlockSpec((1, tm, width), lambda bi, i: (bi, i, z_block)),
                 const((1, width)), const(w_out.shape), row(d),
                 pl.BlockSpec((1, 6, d), lambda bi, i: (jnp.where(i >= nt_lat, 2, bi), 0, 0)),
                 const((1, d)), const(wr.shape), const((1, LANES))]
    args += [o, proj, gnorm, w_out, x_all, mod_l, g2.reshape(1, d), wr, br]
    return pl.pallas_call(
        functools.partial(_out_kernel, group=group, has_a=a is not None),
        grid=(b, t_all // tm),
        in_specs=in_specs,
        out_specs=[row(d), row(d), row(LANES)],
        out_shape=[jax.ShapeDtypeStruct((b, t_all, d), f32), jax.ShapeDtypeStruct((b, t_all, d), f32),
                   jax.ShapeDtypeStruct((b, t_all, LANES), f32)],
        compiler_params=_cparams("parallel", "parallel"),
        name="mixer_output",
    )(*args)


def _moe_kernel(be_ref, tok_ref, nused_ref, h_hbm, wg_ref, wl_ref, bg_ref, bl_ref, wd_ref, bd_ref, sw_ref, y_ref,
                xbuf, x16, acc, sem, *, tm):
    m = pl.program_id(0)
    f = pl.program_id(1)
    nf = pl.num_programs(1)
    active = m < nused_ref[0]

    def row_copy(r):
        tok = tok_ref[m * tm + r]
        return pltpu.make_async_copy(h_hbm.at[pl.ds(tok, 1)], xbuf.at[pl.ds(r, 1)], sem)

    @pl.when(active & (f == 0))
    def _():
        def issue(r, carry):
            row_copy(r).start()
            return carry
        lax.fori_loop(0, tm, issue, 0)
        pltpu.make_async_copy(h_hbm.at[pl.ds(0, tm)], xbuf, sem).wait()
        x16[...] = xbuf[...].astype(bf16)
        acc[...] = jnp.zeros(acc.shape, f32)

    @pl.when(active)
    def _():
        x = x16[...]
        glu = jnp.minimum(_mm(x, wg_ref[0]) + bg_ref[0], SWIGLU_LIMIT)
        lin = jnp.clip(_mm(x, wl_ref[0]) + bl_ref[0], -SWIGLU_LIMIT, SWIGLU_LIMIT)
        act = glu * _sigmoid(SWIGLU_ALPHA * glu) * (lin + 1.0)
        acc[...] += _mm(act.astype(bf16), wd_ref[0])

    @pl.when(active & (f == nf - 1))
    def _():
        y_ref[...] = (acc[...] + bd_ref[0]) * sw_ref[:, 0:1]

    @pl.when(jnp.logical_not(active) & (f == nf - 1))
    def _():
        y_ref[...] = jnp.zeros(y_ref.shape, f32)


def _moe_experts(h2, block_expert, slot_tok, n_used, slot_w, w_gu, b_gu, w_down, b_down):
    n_tok, d = h2.shape
    n_exp, _, ff2 = w_gu.shape
    ff = ff2 // 2
    tm, tf = MOE_ROWS, MOE_FF_TILE
    nf = ff // tf
    n_blocks = block_expert.shape[0]
    p = n_blocks * tm

    def wmap(off):
        def index(m, f, be, tok, nu):
            live = m < nu[0]
            mm = jnp.minimum(m, nu[0] - 1)
            return (be[mm], 0, off + jnp.where(live, f, nf - 1))
        return index

    def dmap(m, f, be, tok, nu):
        live = m < nu[0]
        return (be[jnp.minimum(m, nu[0] - 1)], jnp.where(live, f, nf - 1), 0)

    grid_spec = pltpu.PrefetchScalarGridSpec(
        num_scalar_prefetch=3,
        grid=(n_blocks, nf),
        in_specs=[pl.BlockSpec(memory_space=pl.ANY),
                  pl.BlockSpec((1, d, tf), wmap(0)),
                  pl.BlockSpec((1, d, tf), wmap(nf)),
                  pl.BlockSpec((1, 1, tf), wmap(0)),
                  pl.BlockSpec((1, 1, tf), wmap(nf)),
                  pl.BlockSpec((1, tf, d), dmap),
                  pl.BlockSpec((1, 1, d), lambda m, f, be, tok, nu: (be[jnp.minimum(m, nu[0] - 1)], 0, 0)),
                  pl.BlockSpec((tm, LANES), lambda m, f, be, tok, nu: (m, 0))],
        out_specs=pl.BlockSpec((tm, d), lambda m, f, be, tok, nu: (m, 0)),
        scratch_shapes=[pltpu.VMEM((tm, d), f32), pltpu.VMEM((tm, d), bf16), pltpu.VMEM((tm, d), f32),
                        pltpu.SemaphoreType.DMA(())])
    return pl.pallas_call(
        functools.partial(_moe_kernel, tm=tm),
        grid_spec=grid_spec,
        out_shape=jax.ShapeDtypeStruct((p, d), f32),
        compiler_params=_cparams("arbitrary", "arbitrary"),
        name="moe_experts",
    )(block_expert, slot_tok, n_used, h2, w_gu, w_gu, b_gu.reshape(n_exp, 1, ff2), b_gu.reshape(n_exp, 1, ff2),
      w_down, b_down.reshape(n_exp, 1, d), slot_w)


def _combine_kernel(pos_ref, y_hbm, x_ref, mod_ref, o_ref, buf, sem, *, tb, t_all):
    bi = pl.program_id(0)
    i = pl.program_id(1)
    base = (bi * t_all + i * tb) * TOP_K

    def issue(r, carry):
        for k in range(TOP_K):
            slot = pos_ref[base + r * TOP_K + k]
            pltpu.make_async_copy(y_hbm.at[pl.ds(slot, 1)], buf.at[k, pl.ds(r, 1)], sem).start()
        return carry

    lax.fori_loop(0, tb, issue, 0)
    for k in range(TOP_K):
        pltpu.make_async_copy(y_hbm.at[pl.ds(0, tb)], buf.at[k], sem).wait()
    f = buf[0] + buf[1] + buf[2] + buf[3]
    o_ref[0] = x_ref[0] + mod_ref[0, 5:6, :] * f


def _moe_combine(pos, y_sorted, x_all, mod_l, n_lat, n_rows_out):
    b, t_all, d = x_all.shape
    tb = SEQ_BLOCK
    nt_lat = n_lat // tb
    grid_spec = pltpu.PrefetchScalarGridSpec(
        num_scalar_prefetch=1,
        grid=(b, n_rows_out // tb),
        in_specs=[pl.BlockSpec(memory_space=pl.ANY),
                  pl.BlockSpec((1, tb, d), lambda bi, i, pos: (bi, i, 0)),
                  pl.BlockSpec((1, 6, d), lambda bi, i, pos: (jnp.where(i >= nt_lat, 2, bi), 0, 0))],
        out_specs=pl.BlockSpec((1, tb, d), lambda bi, i, pos: (bi, i, 0)),
        scratch_shapes=[pltpu.VMEM((TOP_K, tb, d), f32), pltpu.SemaphoreType.DMA(())])
    return pl.pallas_call(
        functools.partial(_combine_kernel, tb=tb, t_all=t_all),
        grid_spec=grid_spec,
        out_shape=jax.ShapeDtypeStruct((b, n_rows_out, d), f32),
        compiler_params=_cparams("arbitrary", "arbitrary"),
        name="moe_combine",
    )(pos, y_sorted, x_all, mod_l)


def _route(logits, n_exp):
    n = logits.shape[0]
    tm = MOE_ROWS
    top_val, top_idx = lax.top_k(logits, TOP_K)
    gate = jax.nn.softmax(top_val, axis=-1)
    e_flat = top_idx.reshape(-1).astype(jnp.int32)
    tok_flat = jnp.repeat(jnp.arange(n, dtype=jnp.int32), TOP_K)
    order = jnp.argsort(e_flat)
    e_sorted = e_flat[order]
    counts = jnp.bincount(e_flat, length=n_exp).astype(jnp.int32)
    padded = (counts + tm - 1) // tm * tm
    start = jnp.cumsum(counts) - counts
    pad_end = jnp.cumsum(padded)
    pad_start = pad_end - padded
    dest = (pad_start[e_sorted] + jnp.arange(n * TOP_K, dtype=jnp.int32) - start[e_sorted]).astype(jnp.int32)
    n_blocks = -(-(n * TOP_K) // tm) + n_exp
    p = n_blocks * tm
    slot_tok = jnp.zeros((p,), jnp.int32).at[dest].set(tok_flat[order])
    slot_w = jnp.zeros((p,), f32).at[dest].set(gate.reshape(-1)[order])
    pos = jnp.zeros((n * TOP_K,), jnp.int32).at[order].set(dest)
    block_expert = jnp.minimum(jnp.searchsorted(pad_end, jnp.arange(n_blocks, dtype=jnp.int32) * tm, side='right'),
                               n_exp - 1).astype(jnp.int32)
    n_used = (pad_end[-1] // tm).astype(jnp.int32).reshape(1)
    return block_expert, slot_tok, n_used, jnp.broadcast_to(slot_w[:, None], (p, LANES)), pos


def _rope_perm():
    q = A_HEAD_DIM // 4
    one = np.concatenate([np.arange(0, q), np.arange(2 * q, 3 * q), np.arange(q, 2 * q), np.arange(3 * q, 4 * q)])
    return one


def _rope_tables(n_lat, n_ctx):
    rows = n_lat // GRID_W
    row = jnp.repeat(jnp.arange(rows, dtype=f32), GRID_W)
    col = jnp.tile(jnp.arange(GRID_W, dtype=f32), rows)
    n_freq = A_HEAD_DIM // 4
    inv = ROPE_BASE ** (-jnp.arange(n_freq, dtype=f32) / n_freq)
    ang = jnp.concatenate([row[:, None] * inv, col[:, None] * inv], axis=1)
    cos, sin = jnp.cos(ang), jnp.sin(ang)
    cos_t = jnp.tile(cos, (1, 4))
    sin_t = jnp.tile(jnp.concatenate([-sin, sin], axis=1), (1, 2))
    cos_t = jnp.concatenate([cos_t, jnp.ones((n_ctx, LANES), f32)], axis=0)
    sin_t = jnp.concatenate([sin_t, jnp.zeros((n_ctx, LANES), f32)], axis=0)
    return cos_t, sin_t


def _pad_cols(w, n):
    return jnp.pad(w, ((0, 0), (0, n - w.shape[1])))


def _moe_layer(h2, logits, x_all, mod_l, w_gu16, b_gu, w_down16, b_down, n_lat, n_rows_out):
    b, t_all, d = x_all.shape
    n_exp = w_gu16.shape[0]
    routed = _route(logits.reshape(b * t_all, LANES)[:, :n_exp], n_exp)
    block_expert, slot_tok, n_used, slot_w, pos = routed
    y_sorted = _moe_experts(h2.reshape(b * t_all, d), block_expert, slot_tok, n_used, slot_w,
                            w_gu16, b_gu, w_down16, b_down)
    return _moe_combine(pos, y_sorted, x_all, mod_l, n_lat, n_rows_out)


def kernel(x, c, ctx, c_ctx, w_mod, b_mod, norm_g, e_w_in, e_w_out, e_q_gain, e_k_gain, e_sinks, e_conv_w, e_a_log, e_dt_bias, e_dn_norm, o_w_in, o_gate_w2, o_gate_b, o_gla_norm, o_w_out, w_router, b_router, w_gu, b_gu, w_down, b_down):
    b, s, d = x.shape
    n_ctx = ctx.shape[1]
    depth = w_mod.shape[0]
    n_exp = w_router.shape[2]
    assert n_ctx == SEQ_BLOCK and s % SEQ_BLOCK == 0 and b == 2
    x_all = jnp.concatenate([x, ctx], axis=1)
    t_all = s + n_ctx

    c8 = jnp.concatenate([c, c_ctx[None, :], jnp.zeros((8 - b - 1, d), f32)], axis=0)
    mod = _mod_vectors(c8, w_mod, b_mod).reshape(depth, 8, 6, d)
    wr = jnp.pad(w_router, ((0, 0), (0, 0), (0, LANES - n_exp)))
    br = jnp.pad(b_router, ((0, 0), (0, LANES - n_exp))).reshape(depth, 1, LANES)
    cos_t, sin_t = _rope_tables(s, n_ctx)
    perm = _rope_perm()
    aq = A_HEADS * A_HEAD_DIM
    akv = A_KV_HEADS * A_HEAD_DIM
    bw = B_HEADS * B_HEAD_DIM

    for layer in range(depth):
        i = layer // 2
        mod_l = mod[layer]
        last = layer == depth - 1
        if layer % 2 == 0:
            w = e_w_in[i]
            qcols = (np.arange(A_HEADS)[:, None] * A_HEAD_DIM + perm[None, :]).reshape(-1)
            kcols = aq + (np.arange(A_KV_HEADS)[:, None] * A_HEAD_DIM + perm[None, :]).reshape(-1)
            o_dn = aq + 2 * akv
            o_z = o_dn + 3 * bw
            o_g = o_z + bw
            w_cat = jnp.concatenate([w[:, qcols], w[:, o_z:o_g], w[:, o_dn:o_z], w[:, kcols],
                                     w[:, aq + akv:aq + 2 * akv], w[:, o_g:]], axis=1)
            n_pad = 5760
            w16 = _pad_cols(w_cat, n_pad).astype(bf16)
            proj = _input_projection(x_all, norm_g[layer, 0], mod_l, w16, s, 1152)
            gain = jnp.concatenate([jnp.tile(e_q_gain[i][perm], A_HEADS) * (A_HEAD_DIM ** -0.5),
                                    jnp.tile(e_k_gain[i][perm], A_KV_HEADS)]).reshape(1, aq + akv)
            q16, kt16, vt16 = _qk_prepare(proj, cos_t, sin_t, gain, (2 * bw + 3 * bw) // 512)
            a_all = _attention(e_sinks[i], q16, kt16, vt16, s)
            conv8 = jnp.pad(e_conv_w[i], ((0, 8 - CONV_W), (0, 0)))
            qkv = _dn_prepare(proj, conv8, s, 2)
            gate_block = (n_pad - LANES) // LANES
            gates_t = jnp.swapaxes(proj[:, :, n_pad - LANES:n_pad - LANES + 4 * B_HEADS], 1, 2)
            o_f = _dn_scan(qkv, proj, gates_t, e_a_log[i], e_dt_bias[i], None, s, gate_block, rev=False)
            o_mix = _dn_scan(qkv, proj, gates_t, e_a_log[i], e_dt_bias[i], o_f, s, gate_block, rev=True)
            gnorm = jnp.tile(e_dn_norm[i], B_HEADS).reshape(1, bw)
            x_all, h2, logits = _mixer_output(a_all, o_mix, proj, 1, gnorm, e_w_out[i].astype(bf16), x_all, mod_l,
                                              norm_g[layer, 1], wr[layer], br[layer], s, B_HEAD_DIM)
        else:
            w = o_w_in[i]
            n_pad = 6272
            proj = _input_projection(x_all, norm_g[layer, 0], mod_l, _pad_cols(w, n_pad).astype(bf16), s, 896)
            dk_all = o_gate_w2.shape[3]
            o_mix = None
            for dirn in range(2):
                w2pad = jnp.zeros((LANES, dk_all), f32).at[dirn * GATE_RANK:(dirn + 1) * GATE_RANK].set(
                    o_gate_w2[i, dirn])
                o_mix = _gla_scan(proj, w2pad, o_gate_b[i, dirn].reshape(1, dk_all), o_mix, s, rev=dirn == 1)
            dv = o_gla_norm.shape[1]
            gnorm = jnp.tile(o_gla_norm[i], C_HEADS).reshape(1, C_HEADS * dv)
            x_all, h2, logits = _mixer_output(None, o_mix, proj, 2, gnorm, o_w_out[i].astype(bf16), x_all, mod_l,
                                              norm_g[layer, 1], wr[layer], br[layer], s, dv)
        x_all = _moe_layer(h2, logits, x_all, mod_l, w_gu[layer].astype(bf16), b_gu[layer],
                           w_down[layer].astype(bf16), b_down[layer], s, s if last else t_all)
    return x_all
```

```python
import functools

import numpy as np
import jax
import jax.numpy as jnp
from jax import lax
from jax.experimental import pallas as pl
from jax.experimental.pallas import tpu as pltpu

f32 = jnp.float32
bf16 = jnp.bfloat16

NORM_EPS = 1e-6
NEG_INF = -1e30
GRID_W = 64
ROPE_BASE = 10000.0
A_HEADS, A_KV_HEADS, A_HEAD_DIM, WINDOW = 16, 4, 64, 128
A_GROUP = A_HEADS // A_KV_HEADS
B_HEADS, B_HEAD_DIM, CONV_W = 8, 128, 5
CHUNK = 64
DN_NEWTON_STEPS = 2
C_HEADS = 4
GATE_RANK = 16
GATE_NORMALIZER = 16.0
TOP_K = 4
SWIGLU_ALPHA, SWIGLU_LIMIT = 1.702, 7.0

LANES = 128
SEQ_BLOCK = 256
VMEM_LIMIT = 56 * 1024 * 1024
MOE_ROWS = 512
MOE_FF_TILE = 512


def _cparams(*sem):
    return pltpu.CompilerParams(dimension_semantics=sem, vmem_limit_bytes=VMEM_LIMIT)


def _split2(x):
    hi = x.astype(bf16)
    return hi, (x - hi.astype(f32)).astype(bf16)


def _split3(x):
    x1 = x.astype(bf16)
    r1 = x - x1.astype(f32)
    x2 = r1.astype(bf16)
    x3 = (r1 - x2.astype(f32)).astype(bf16)
    return x1, x2, x3


def _mm(a, b):
    return jnp.dot(a, b, preferred_element_type=f32)


def _mm_nt(a, b):
    return lax.dot_general(a, b, (((1,), (1,)), ((), ())), preferred_element_type=f32)


def _mm_lhs01(a01, x):
    x1, x2, x3 = _split3(x)
    return _mm(a01, x1) + _mm(a01, x2) + _mm(a01, x3)


def _mm_rhs01(x, b01):
    x1, x2, x3 = _split3(x)
    return _mm(x1, b01) + _mm(x2, b01) + _mm(x3, b01)


def _mm_x3(a, b):
    a1, a2 = _split2(a)
    b1, b2 = _split2(b)
    return _mm(a1, b1) + _mm(a1, b2) + _mm(a2, b1)


def _sigmoid(x):
    return 1.0 / (1.0 + jnp.exp(-x))


def _softplus(x):
    return jnp.maximum(x, 0.0) + jnp.log(1.0 + jnp.exp(-jnp.abs(x)))


def _mask01(m):
    return jnp.where(m, 1.0, 0.0).astype(bf16)


def _chunk_masks(n, rev):
    row = lax.broadcasted_iota(jnp.int32, (n, n), 0)
    col = lax.broadcasted_iota(jnp.int32, (n, n), 1)
    same = (row // CHUNK) == (col // CHUNK)
    if rev:
        return same, same & (col >= row), same & (col > row), row, col
    return same, same & (col <= row), same & (col < row), row, col


def _mod_kernel(c_ref, w_ref, b_ref, o_ref):
    c = c_ref[...]
    o_ref[0] = _mm_x3(c * _sigmoid(c), w_ref[0]) + b_ref[0]


def _mod_vectors(c8, w_mod, b_mod):
    depth, d, n = w_mod.shape
    tn = 1024
    return pl.pallas_call(
        _mod_kernel,
        grid=(depth, n // tn),
        in_specs=[pl.BlockSpec((8, d), lambda l, j: (0, 0)),
                  pl.BlockSpec((1, d, tn), lambda l, j: (l, 0, j)),
                  pl.BlockSpec((1, 1, tn), lambda l, j: (l, 0, j))],
        out_specs=pl.BlockSpec((1, 8, tn), lambda l, j: (l, 0, j)),
        out_shape=jax.ShapeDtypeStruct((depth, 8, n), f32),
        compiler_params=_cparams("parallel", "parallel"),
        name="mod_vectors",
    )(c8, w_mod, b_mod.reshape(depth, 1, n))


def _proj_kernel(x_ref, g_ref, mb_ref, mc_ref, w_ref, o_ref, h_ref, *, tm, n_lat, sub):
    i = pl.program_id(1)

    @pl.when(pl.program_id(2) == 0)
    def _():
        for r0 in range(0, tm, sub):
            x = x_ref[0, r0:r0 + sub, :]
            y = x * lax.rsqrt(jnp.mean(x * x, axis=-1, keepdims=True) + NORM_EPS) * g_ref[...]
            row = i * tm + r0 + lax.broadcasted_iota(jnp.int32, (sub, 1), 0)
            is_ctx = row >= n_lat
            shift = jnp.where(is_ctx, mc_ref[0, 0:1, :], mb_ref[0, 0:1, :])
            scale = jnp.where(is_ctx, mc_ref[0, 1:2, :], mb_ref[0, 1:2, :])
            h_ref[r0:r0 + sub, :] = (y * (1.0 + scale) + shift).astype(bf16)

    o_ref[0] = _mm(h_ref[...], w_ref[...])


def _input_projection(x_all, gain, mod_l, w, n_lat, tn):
    b, t_all, d = x_all.shape
    n = w.shape[1]
    tm = 1280 if t_all % 1280 == 0 else SEQ_BLOCK
    assert t_all % tm == 0 and n % tn == 0
    return pl.pallas_call(
        functools.partial(_proj_kernel, tm=tm, n_lat=n_lat, sub=128),
        grid=(b, t_all // tm, n // tn),
        in_specs=[pl.BlockSpec((1, tm, d), lambda bi, i, j: (bi, i, 0)),
                  pl.BlockSpec((1, d), lambda bi, i, j: (0, 0)),
                  pl.BlockSpec((1, 6, d), lambda bi, i, j: (bi, 0, 0)),
                  pl.BlockSpec((1, 6, d), lambda bi, i, j: (2, 0, 0)),
                  pl.BlockSpec((d, tn), lambda bi, i, j: (0, j))],
        out_specs=pl.BlockSpec((1, tm, tn), lambda bi, i, j: (bi, i, j)),
        out_shape=jax.ShapeDtypeStruct((b, t_all, n), f32),
        scratch_shapes=[pltpu.VMEM((tm, d), bf16)],
        compiler_params=_cparams("parallel", "parallel", "arbitrary"),
        name="input_projection",
    )(x_all, gain.reshape(1, d), mod_l, mod_l, w)


def _qkprep_kernel(q_ref, kv_ref, cos_ref, sin_ref, gain_ref, ones_ref, e_ref, qo_ref, ko_ref, vo_ref):
    cos = cos_ref[...]
    sin = sin_ref[...]
    lane = lax.broadcasted_iota(jnp.int32, cos.shape, 1)
    upper = (lane & 32) != 0
    ones = ones_ref[...]

    def norm_rope(x, gain):
        ms = _mm_rhs01(x * x, ones) * (1.0 / A_HEAD_DIM)
        y = x * lax.rsqrt(ms + NORM_EPS) * gain
        outs = []
        for c in range(2):
            yc = y[:, c * LANES:(c + 1) * LANES]
            partner = jnp.where(upper, pltpu.roll(yc, 32, 1), pltpu.roll(yc, 96, 1))
            outs.append(yc * cos + partner * sin)
        return jnp.concatenate(outs, axis=1)

    for c in range(4):
        sl = slice(c * 256, (c + 1) * 256)
        qo_ref[0, :, sl] = norm_rope(q_ref[0, :, sl], gain_ref[:, sl]).astype(bf16)
    k = norm_rope(kv_ref[0, :, 0:256], gain_ref[:, 1024:1280]).astype(bf16)
    ko_ref[0] = _mm(k, e_ref[...]).astype(bf16)
    vo_ref[0] = _mm(kv_ref[0, :, 256:512].astype(bf16), e_ref[...]).astype(bf16)


def _qk_prepare(proj, cos_t, sin_t, gain, kv_block):
    b, t_all, _ = proj.shape
    tt = SEQ_BLOCK
    hd = A_HEAD_DIM
    ones = np.kron(np.eye(4, dtype=np.float32), np.ones((hd, hd), np.float32))
    expand = np.kron(np.eye(4, dtype=np.float32), np.tile(np.eye(hd, dtype=np.float32), (1, A_GROUP)))
    out = jax.ShapeDtypeStruct((b, t_all, 1024), bf16)
    return pl.pallas_call(
        _qkprep_kernel,
        grid=(b, t_all // tt),
        in_specs=[pl.BlockSpec((1, tt, 1024), lambda bi, i: (bi, i, 0)),
                  pl.BlockSpec((1, tt, 512), lambda bi, i: (bi, i, kv_block)),
                  pl.BlockSpec((tt, LANES), lambda bi, i: (i, 0)),
                  pl.BlockSpec((tt, LANES), lambda bi, i: (i, 0)),
                  pl.BlockSpec((1, 1280), lambda bi, i: (0, 0)),
                  pl.BlockSpec((256, 256), lambda bi, i: (0, 0)),
                  pl.BlockSpec((256, 1024), lambda bi, i: (0, 0))],
        out_specs=[pl.BlockSpec((1, tt, 1024), lambda bi, i: (bi, i, 0))] * 3,
        out_shape=[out, out, out],
        compiler_params=_cparams("parallel", "parallel"),
        name="qk_prepare",
    )(proj, proj, cos_t, sin_t, gain, jnp.asarray(ones, bf16), jnp.asarray(expand, bf16))


def _attn_kernel(sink_ref, q_ref, kp_ref, kc_ref, kn_ref, kx_ref, vp_ref, vc_ref, vn_ref, vx_ref, o_ref, *,
                 nq_lat, n_ctx):
    n = pl.program_id(1)
    blk = WINDOW
    rows = A_GROUP * blk
    nband = 3 * blk
    ncol = nband + n_ctx
    rq = lax.broadcasted_iota(jnp.int32, (rows, ncol), 0) % blk
    col = lax.broadcasted_iota(jnp.int32, (rows, ncol), 1)
    rgrp = lax.broadcasted_iota(jnp.int32, (rows, 1), 0) // blk
    dist = col - rq
    is_lat = n < nq_lat
    lo = jnp.where(n > 0, 0, blk)
    hi = jnp.where(is_lat, jnp.where(n < nq_lat - 1, nband, 2 * blk), 0)
    allowed = (col >= nband) | ((dist >= 0) & (dist <= 2 * WINDOW) & (col >= lo) & (col < hi))
    lane_g = lax.broadcasted_iota(jnp.int32, (blk, A_GROUP * A_HEAD_DIM), 1) // A_HEAD_DIM
    for h in range(A_KV_HEADS):
        sl = slice(h * 256, (h + 1) * 256)
        qh = q_ref[0, :, sl]
        qs = jnp.concatenate([jnp.where(lane_g == g, qh, jnp.zeros_like(qh)) for g in range(A_GROUP)], axis=0)
        kcat = jnp.concatenate([kp_ref[0, :, sl], kc_ref[0, :, sl], kn_ref[0, :, sl], kx_ref[0, :, sl]], axis=0)
        vcat = jnp.concatenate([vp_ref[0, :, sl], vc_ref[0, :, sl], vn_ref[0, :, sl], vx_ref[0, :, sl]], axis=0)
        s = jnp.where(allowed, _mm_nt(qs, kcat), NEG_INF)
        sink = jnp.zeros((rows, 1), f32)
        for g in range(A_GROUP):
            sink = jnp.where(rgrp == g, sink_ref[h * A_GROUP + g], sink)
        m = jnp.maximum(jnp.max(s, axis=-1, keepdims=True), sink)
        p = jnp.exp(s - m)
        den = jnp.sum(p, axis=-1, keepdims=True) + jnp.exp(sink - m)
        o = _mm(p.astype(bf16), vcat) * (1.0 / den)
        out = jnp.zeros((blk, 256), f32)
        for g in range(A_GROUP):
            out = out + jnp.where(lane_g == g, o[g * blk:(g + 1) * blk, :], 0.0)
        o_ref[0, :, sl] = out.astype(bf16)


def _attention(sinks, q, kt, vt, n_lat):
    b, t_all, w = q.shape
    blk = WINDOW
    nq = t_all // blk
    n_ctx = t_all - n_lat
    ctx_blk = n_lat // n_ctx
    qspec = pl.BlockSpec((1, blk, w), lambda bi, i: (bi, i, 0))
    prev = pl.BlockSpec((1, blk, w), lambda bi, i: (bi, jnp.maximum(i - 1, 0), 0))
    nxt = pl.BlockSpec((1, blk, w), lambda bi, i: (bi, jnp.minimum(i + 1, nq - 1), 0))
    ctx = pl.BlockSpec((1, n_ctx, w), lambda bi, i: (bi, ctx_blk, 0))
    return pl.pallas_call(
        functools.partial(_attn_kernel, nq_lat=n_lat // blk, n_ctx=n_ctx),
        grid=(b, nq),
        in_specs=[pl.BlockSpec(memory_space=pltpu.SMEM), qspec, prev, qspec, nxt, ctx, prev, qspec, nxt, ctx],
        out_specs=qspec,
        out_shape=jax.ShapeDtypeStruct((b, t_all, w), bf16),
        compiler_params=_cparams("parallel", "parallel"),
        name="window_attention",
    )(sinks, q, kt, kt, kt, kt, vt, vt, vt, vt)


def _dnprep_kernel(x_ref, xp_ref, xn_ref, w_ref, o_ref, ext_ref, *, nt_lat, tt):
    i = pl.program_id(1)
    j = pl.program_id(2)
    keep_prev = jnp.where((i == 0) | (i == nt_lat), 0.0, 1.0)
    keep_next = jnp.where((i == nt_lat - 1) | (i == nt_lat), 0.0, 1.0)
    ext_ref[0:8, :] = xp_ref[0] * keep_prev
    ext_ref[8:8 + tt, :] = x_ref[0]
    ext_ref[8 + tt:16 + tt, :] = xn_ref[0] * keep_next
    half = CONV_W // 2
    acc = jnp.zeros((tt, x_ref.shape[2]), f32)
    for tap in range(CONV_W):
        acc = acc + ext_ref[pl.ds(8 - half + tap, tt), :] * w_ref[tap:tap + 1, :]
    y = acc * _sigmoid(acc)
    q_scale = jnp.where(j == 0, B_HEAD_DIM ** -0.5, 1.0)
    for h in range(x_ref.shape[2] // B_HEAD_DIM):
        sl = slice(h * B_HEAD_DIM, (h + 1) * B_HEAD_DIM)
        yh = y[:, sl]
        rn = lax.rsqrt(jnp.sum(yh * yh, axis=-1, keepdims=True) + 1e-6) * q_scale
        o_ref[0, :, sl] = yh * jnp.where(j == 2, 1.0, rn)


def _dn_prepare(proj, conv_w8, n_lat, col_block):
    b, t_all, _ = proj.shape
    tt = SEQ_BLOCK
    bw = B_HEADS * B_HEAD_DIM
    r8 = tt // 8
    return pl.pallas_call(
        functools.partial(_dnprep_kernel, nt_lat=n_lat // tt, tt=tt),
        grid=(b, t_all // tt, 3),
        in_specs=[pl.BlockSpec((1, tt, bw), lambda bi, i, j: (bi, i, col_block + j)),
                  pl.BlockSpec((1, 8, bw), lambda bi, i, j: (bi, jnp.maximum(i * r8 - 1, 0), col_block + j)),
                  pl.BlockSpec((1, 8, bw), lambda bi, i, j: (bi, jnp.minimum((i + 1) * r8, t_all // 8 - 1),
                                                             col_block + j)),
                  pl.BlockSpec((8, bw), lambda bi, i, j: (0, j))],
        out_specs=pl.BlockSpec((1, tt, bw), lambda bi, i, j: (bi, i, j)),
        out_shape=jax.ShapeDtypeStruct((b, t_all, 3 * bw), f32),
        scratch_shapes=[pltpu.VMEM((tt + 16, bw), f32)],
        compiler_params=_cparams("parallel", "parallel", "parallel"),
        name="deltanet_prepare",
    )(proj, proj, proj, conv_w8)


def _dnscan_kernel(*refs, rev, add_prev):
    if add_prev:
        (q_ref, k_ref, v_ref, gc_ref, gt_ref, alr_ref, dtr_ref, alc_ref, dtc_ref, op_ref, o_ref,
         s_ref, u_scr, w_scr, qd_scr, kdt_scr, at_scr, vn_scr, egl_scr) = refs
    else:
        (q_ref, k_ref, v_ref, gc_ref, gt_ref, alr_ref, dtr_ref, alc_ref, dtc_ref, o_ref,
         s_ref, u_scr, w_scr, qd_scr, kdt_scr, at_scr, vn_scr, egl_scr) = refs
        op_ref = None
    tt = q_ref.shape[1]
    hd = B_HEAD_DIM
    d = 1 if rev else 0

    @pl.when(pl.program_id(1) == 0)
    def _():
        s_ref[...] = jnp.zeros(s_ref.shape, f32)

    same, incl, strict, row, col = _chunk_masks(tt, rev)
    tri = _mask01(incl)
    tri_t = _mask01(same & ((row >= col) if rev else (row <= col)))
    ones_bd = _mask01(same)
    eye = jnp.where(row == col, 1.0, 0.0)

    gates = gc_ref[0]
    g_all = -jnp.exp(alr_ref[...]) * _softplus(gates + dtr_ref[...])
    beta_all = _sigmoid(gates)
    gcum = _mm_lhs01(tri, g_all)
    gsum = _mm_lhs01(ones_bd, g_all)
    egl_scr[...] = jnp.exp(gsum)
    g_row = -jnp.exp(alc_ref[...]) * _softplus(gt_ref[0] + dtc_ref[...])
    gcum_r = _mm_rhs01(g_row, tri_t)

    ms, ps, rhss, m0s = [], [], [], []
    for h in range(B_HEADS):
        li = d * B_HEADS + h
        sl = slice(h * hd, (h + 1) * hd)
        qh, kh, vh = q_ref[0, :, sl], k_ref[0, :, sl], v_ref[0, :, sl]
        gc = gcum[:, li:li + 1]
        gs = gsum[:, li:li + 1]
        beta = beta_all[:, 2 * B_HEADS + li:2 * B_HEADS + li + 1]
        dec = jnp.where(incl, jnp.exp(jnp.where(incl, gc - gcum_r[li:li + 1, :], 0.0)), 0.0)
        kb = kh * beta
        kh16 = kh.astype(bf16)
        m = -jnp.where(strict, _mm_nt(kb.astype(bf16), kh16) * dec, 0.0)
        ms.append(m)
        m0s.append(_split2(m))
        ps.append(eye + m)
        egc = jnp.exp(gc)
        rhss.append(_split2(jnp.concatenate([vh * beta, kb * egc], axis=1)))
        at_scr[h] = jnp.where(incl, _mm_nt(qh.astype(bf16), kh16) * dec, 0.0)
        qd_scr[h] = qh * egc
        kdt_scr[h] = (kh * jnp.exp(gs - gc)).T
    for _ in range(5):
        for h in range(B_HEADS):
            m16 = ms[h].astype(bf16)
            ms[h] = _mm(m16, m16)
        for h in range(B_HEADS):
            ps[h] = ps[h] + _mm(ps[h].astype(bf16), ms[h].astype(bf16))
    for _ in range(DN_NEWTON_STEPS):
        for h in range(B_HEADS):
            m_hi, m_lo = m0s[h]
            x_hi, x_lo = _split2(ps[h])
            resid = (eye - ps[h]) + (_mm(m_hi, x_hi) + _mm(m_hi, x_lo) + _mm(m_lo, x_hi))
            ps[h] = ps[h] + _mm(x_hi, resid.astype(bf16))
    for h in range(B_HEADS):
        p_hi, p_lo = _split2(ps[h])
        r_hi, r_lo = rhss[h]
        sol = _mm(p_hi, r_hi) + _mm(p_hi, r_lo) + _mm(p_lo, r_hi)
        u_scr[h] = sol[:, :hd]
        vn_scr[h] = sol[:, :hd]
        w_scr[h] = sol[:, hd:]

    colk = lax.broadcasted_iota(jnp.int32, (hd, tt), 1) // CHUNK
    nchunk = tt // CHUNK
    for c in (range(nchunk - 1, -1, -1) if rev else range(nchunk)):
        rs = slice(c * CHUNK, (c + 1) * CHUNK)
        egl_row = egl_scr[c * CHUNK:c * CHUNK + 1, :]
        for h in range(B_HEADS):
            li = d * B_HEADS + h
            sl = slice(h * hd, (h + 1) * hd)
            s = s_ref[h]
            s16 = s.astype(bf16)
            vnew = u_scr[h, rs, :] - _mm(w_scr[h, rs, :].astype(bf16), s16)
            vn_scr[h, rs, :] = vnew
            vn16 = vn_scr[h].astype(bf16)
            o = _mm(qd_scr[h, rs, :].astype(bf16), s16) + _mm(at_scr[h, rs, :].astype(bf16), vn16)
            if op_ref is not None:
                o = o + op_ref[0, rs, sl]
            o_ref[0, rs, sl] = o
            kdt = jnp.where(colk == c, kdt_scr[h], 0.0).astype(bf16)
            s_ref[h] = s * egl_row[:, li:li + 1] + _mm(kdt, vn16)


def _seq_block_index(i, nt_lat, rev):
    lat = (nt_lat - i) if rev else (i - 1)
    return jnp.where(i == 0, nt_lat, lat)


def _dn_scan(qkv, proj, gates_t, alog, dtb, o_prev, n_lat, gate_block, rev):
    b, t_all, _ = qkv.shape
    tt = SEQ_BLOCK
    nt_lat = n_lat // tt
    bw = B_HEADS * B_HEAD_DIM
    hd = B_HEAD_DIM
    blk = lambda bi, i: _seq_block_index(i, nt_lat, rev)
    pad = jnp.zeros((LANES - 2 * B_HEADS,), f32)
    alr = jnp.concatenate([alog.reshape(-1), pad]).reshape(1, LANES)
    dtr = jnp.concatenate([dtb.reshape(-1), pad]).reshape(1, LANES)
    alc = jnp.concatenate([alog.reshape(-1), pad[:2 * B_HEADS]]).reshape(4 * B_HEADS, 1)
    dtc = jnp.concatenate([dtb.reshape(-1), pad[:2 * B_HEADS]]).reshape(4 * B_HEADS, 1)
    small = lambda shape: pl.BlockSpec(shape, lambda bi, i: (0, 0))
    in_specs = [pl.BlockSpec((1, tt, bw), lambda bi, i: (bi, blk(bi, i), 0)),
                pl.BlockSpec((1, tt, bw), lambda bi, i: (bi, blk(bi, i), 1)),
                pl.BlockSpec((1, tt, bw), lambda bi, i: (bi, blk(bi, i), 2)),
                pl.BlockSpec((1, tt, LANES), lambda bi, i: (bi, blk(bi, i), gate_block)),
                pl.BlockSpec((1, 4 * B_HEADS, tt), lambda bi, i: (bi, 0, blk(bi, i))),
                small((1, LANES)), small((1, LANES)), small((4 * B_HEADS, 1)), small((4 * B_HEADS, 1))]
    args = [qkv, qkv, qkv, proj, gates_t, alr, dtr, alc, dtc]
    if o_prev is not None:
        in_specs.append(pl.BlockSpec((1, tt, bw), lambda bi, i: (bi, blk(bi, i), 0)))
        args.append(o_prev)
    head = lambda *s: pltpu.VMEM((B_HEADS,) + s, f32)
    return pl.pallas_call(
        functools.partial(_dnscan_kernel, rev=rev, add_prev=o_prev is not None),
        grid=(b, nt_lat + 1),
        in_specs=in_specs,
        out_specs=pl.BlockSpec((1, tt, bw), lambda bi, i: (bi, blk(bi, i), 0)),
        out_shape=jax.ShapeDtypeStruct((b, t_all, bw), f32),
        scratch_shapes=[head(hd, hd), head(tt, hd), head(tt, hd), head(tt, hd), head(hd, tt), head(tt, tt),
                        head(tt, hd), pltpu.VMEM((tt, LANES), f32)],
        compiler_params=_cparams("parallel", "arbitrary"),
        name="deltanet_scan_bwd" if rev else "deltanet_scan_fwd",
    )(*args)


def _glascan_kernel(*refs, rev, add_prev):
    if add_prev:
        q_ref, k_ref, v_ref, r_ref, w2_ref, gb_ref, op_ref, o_ref, s_ref = refs
    else:
        q_ref, k_ref, v_ref, r_ref, w2_ref, gb_ref, o_ref, s_ref = refs
        op_ref = None
    tt, dk = q_ref.shape[1], q_ref.shape[2]

    @pl.when(pl.program_id(2) == 0)
    def _():
        s_ref[...] = jnp.zeros(s_ref.shape, f32)

    same, incl, _, _, _ = _chunk_masks(tt, rev)
    tri = _mask01(incl)
    ones_bd = _mask01(same)
    z = _mm_x3(r_ref[0], w2_ref[...]) + gb_ref[...]
    gk = (jnp.minimum(z, 0.0) - jnp.log(1.0 + jnp.exp(-jnp.abs(z)))) * (1.0 / GATE_NORMALIZER)
    bcum = _mm_lhs01(tri, gk)
    bsum = _mm_lhs01(ones_bd, gk)
    nchunk = tt // CHUNK
    mid = (CHUNK - 1 - CHUNK // 2) if rev else CHUNK // 2
    bmid = jnp.concatenate(
        [jnp.broadcast_to(bcum[c * CHUNK + mid:c * CHUNK + mid + 1, :], (CHUNK, dk)) for c in range(nchunk)], axis=0)
    q = q_ref[0] * (dk ** -0.5)
    k = k_ref[0]
    v16 = v_ref[0].astype(bf16)
    attn = jnp.where(incl, _mm_nt((q * jnp.exp(bcum - bmid)).astype(bf16),
                                  (k * jnp.exp(bmid - bcum)).astype(bf16)), 0.0)
    o_intra = _mm(attn.astype(bf16), v16)
    q_dec = (q * jnp.exp(bcum)).astype(bf16)
    kd_t = (k * jnp.exp(bsum - bcum)).T
    dec_t = jnp.exp(bsum).T
    colk = lax.broadcasted_iota(jnp.int32, (dk, tt), 1) // CHUNK
    for c in (range(nchunk - 1, -1, -1) if rev else range(nchunk)):
        rs = slice(c * CHUNK, (c + 1) * CHUNK)
        s = s_ref[...]
        o = o_intra[rs, :] + _mm(q_dec[rs, :], s.astype(bf16))
        if op_ref is not None:
            o = o + op_ref[0, rs, :]
        o_ref[0, rs, :] = o
        kdt = jnp.where(colk == c, kd_t, 0.0).astype(bf16)
        s_ref[...] = s * dec_t[:, c * CHUNK:c * CHUNK + 1] + _mm(kdt, v16)


def _gla_scan(proj, w2pad, gate_b, o_prev, n_lat, rev):
    b, t_all, _ = proj.shape
    tt = SEQ_BLOCK
    nt_lat = n_lat // tt
    dk = w2pad.shape[1] // C_HEADS
    dv = 2 * dk
    nq = C_HEADS * dk // dk
    blk = lambda i: _seq_block_index(i, nt_lat, rev)
    in_specs = [pl.BlockSpec((1, tt, dk), lambda bi, h, i: (bi, blk(i), h)),
                pl.BlockSpec((1, tt, dk), lambda bi, h, i: (bi, blk(i), nq + h)),
                pl.BlockSpec((1, tt, dv), lambda bi, h, i: (bi, blk(i), nq + h)),
                pl.BlockSpec((1, tt, LANES), lambda bi, h, i: (bi, blk(i), 6 * C_HEADS * dk // LANES)),
                pl.BlockSpec((LANES, dk), lambda bi, h, i: (0, h)),
                pl.BlockSpec((1, dk), lambda bi, h, i: (0, h))]
    args = [proj, proj, proj, proj, w2pad, gate_b]
    if o_prev is not None:
        in_specs.append(pl.BlockSpec((1, tt, dv), lambda bi, h, i: (bi, blk(i), h)))
        args.append(o_prev)
    return pl.pallas_call(
        functools.partial(_glascan_kernel, rev=rev, add_prev=o_prev is not None),
        grid=(b, C_HEADS, nt_lat + 1),
        in_specs=in_specs,
        out_specs=pl.BlockSpec((1, tt, dv), lambda bi, h, i: (bi, blk(i), h)),
        out_shape=jax.ShapeDtypeStruct((b, t_all, C_HEADS * dv), f32),
        scratch_shapes=[pltpu.VMEM((dk, dv), f32)],
        compiler_params=_cparams("parallel", "parallel", "arbitrary"),
        name="gla_scan_bwd" if rev else "gla_scan_fwd",
    )(*args)


def _out_kernel(*refs, group, has_a):
    if has_a:
        a_ref, o_ref, z_ref, gn_ref, w_ref, x_ref, mod_ref, g2_ref, wr_ref, br_ref, xo_ref, ho_ref, lo_ref = refs
    else:
        o_ref, z_ref, gn_ref, w_ref, x_ref, mod_ref, g2_ref, wr_ref, br_ref, xo_ref, ho_ref, lo_ref = refs
    width = o_ref.shape[2]
    z = z_ref[0]
    gate = z * _sigmoid(z)
    parts = []
    for g0 in range(0, width, group):
        og = o_ref[0, :, g0:g0 + group]
        parts.append(og * lax.rsqrt(jnp.mean(og * og, axis=-1, keepdims=True) + NORM_EPS))
    on = (jnp.concatenate(parts, axis=1) * gn_ref[...] * gate).astype(bf16)
    if has_a:
        na = a_ref.shape[2]
        y = _mm(a_ref[0], w_ref[0:na, :]) + _mm(on, w_ref[na:na + width, :])
    else:
        y = _mm(on, w_ref[...])
    xn = x_ref[0] + mod_ref[0, 2:3, :] * y
    xo_ref[0] = xn
    hn = xn * lax.rsqrt(jnp.mean(xn * xn, axis=-1, keepdims=True) + NORM_EPS) * g2_ref[...]
    h2 = hn * (1.0 + mod_ref[0, 4:5, :]) + mod_ref[0, 3:4, :]
    ho_ref[0] = h2
    lo_ref[0] = _mm_x3(h2, wr_ref[...]) + br_ref[...]


def _mixer_output(a, o, proj, z_block, gnorm, w_out, x_all, mod_l, g2, wr, br, n_lat, group):
    b, t_all, d = x_all.shape
    tm = SEQ_BLOCK
    nt_lat = n_lat // tm
    width = o.shape[2]
    row = lambda w: pl.BlockSpec((1, tm, w), lambda bi, i: (bi, i, 0))
    const = lambda shape: pl.BlockSpec(shape, lambda bi, i: (0,) * len(shape))
    in_specs, args = [], []
    if a is not None:
        in_specs.append(row(a.shape[2]))
        args.append(a)
    in_specs += [row(width),
                 pl.BlockSpec((1, tm, width), lambda bi, i: (bi, i, z_block)),
                 const((1, width)), const(w_out.shape), row(d),
                 pl.BlockSpec((1, 6, d), lambda bi, i: (jnp.where(i >= nt_lat, 2, bi), 0, 0)),
                 const((1, d)), const(wr.shape), const((1, LANES))]
    args += [o, proj, gnorm, w_out, x_all, mod_l, g2.reshape(1, d), wr, br]
    return pl.pallas_call(
        functools.partial(_out_kernel, group=group, has_a=a is not None),
        grid=(b, t_all // tm),
        in_specs=in_specs,
        out_specs=[row(d), row(d), row(LANES)],
        out_shape=[jax.ShapeDtypeStruct((b, t_all, d), f32), jax.ShapeDtypeStruct((b, t_all, d), f32),
                   jax.ShapeDtypeStruct((b, t_all, LANES), f32)],
        compiler_params=_cparams("parallel", "parallel"),
        name="mixer_output",
    )(*args)


def _moe_kernel(be_ref, code_ref, nused_ref, h_hbm, wg_ref, wl_ref, bg_ref, bl_ref, wd_ref, bd_ref, sw_ref, y_hbm,
                xbuf, x16, acc, gsem, ssem, *, tm, nf, n_tok):
    m = pl.program_id(0)
    f = pl.program_id(1)
    nu = nused_ref[0]
    active = m < nu
    cur = m % 2
    oth = 1 - cur
    rows = tm // nf

    def gather_start(blk, buf, r):
        tok = code_ref[blk * tm + r] & 0xFFFF
        pltpu.make_async_copy(h_hbm.at[pl.ds(tok, 1)], xbuf.at[buf, pl.ds(r, 1)], gsem.at[buf]).start()

    def gather_wait(buf):
        pltpu.make_async_copy(h_hbm.at[pl.ds(0, tm)], xbuf.at[buf], gsem.at[buf]).wait()

    def scatter_start(blk, buf, r, real):
        code = code_ref[blk * tm + r]
        dst = ((code >> 16) & 7) * n_tok + (code & 0xFFFF)
        dst = jnp.where(real, dst, TOP_K * n_tok + r)
        pltpu.make_async_copy(acc.at[buf, pl.ds(r, 1)], y_hbm.at[pl.ds(dst, 1)], ssem).start()

    def scatter_wait():
        pltpu.make_async_copy(acc.at[0], y_hbm.at[pl.ds(0, tm)], ssem).wait()

    @pl.when(active & (f == 0))
    def _():
        @pl.when(m == 0)
        def _():
            def issue(r, carry):
                gather_start(0, 0, r)
                return carry
            lax.fori_loop(0, tm, issue, 0)
            acc[1] = jnp.zeros(acc.shape[1:], f32)

        gather_wait(cur)

        @pl.when(m > 0)
        def _():
            scatter_wait()

        x16[...] = xbuf[cur].astype(bf16)
        acc[cur] = jnp.zeros(acc.shape[1:], f32)

    nxt = jnp.minimum(m + 1, pl.num_programs(0) - 1)
    prv = jnp.maximum(m - 1, 0)
    for fs in range(nf):
        @pl.when(active & (f == fs))
        def _(fs=fs):
            for r in range(fs * rows, (fs + 1) * rows):
                gather_start(nxt, oth, r)
                scatter_start(prv, oth, r, m > 0)
            x = x16[...]
            glu = jnp.minimum(_mm(x, wg_ref[0]) + bg_ref[0], SWIGLU_LIMIT)
            lin = jnp.clip(_mm(x, wl_ref[0]) + bl_ref[0], -SWIGLU_LIMIT, SWIGLU_LIMIT)
            act = glu * _sigmoid(SWIGLU_ALPHA * glu) * (lin + 1.0)
            acc[cur] = acc[cur] + _mm(act.astype(bf16), wd_ref[0])

    @pl.when(active & (f == nf - 1))
    def _():
        acc[cur] = (acc[cur] + bd_ref[0]) * sw_ref[...]

    @pl.when((m == nu - 1) & (f == nf - 1))
    def _():
        scatter_wait()

        def issue(r, carry):
            scatter_start(m, cur, r, True)
            return carry
        lax.fori_loop(0, tm, issue, 0)
        scatter_wait()
        gather_wait(oth)


def _moe_experts(h2, block_expert, slot_code, n_used, slot_w, w_gu, b_gu, w_down, b_down, layer):
    n_tok, d = h2.shape
    _, n_exp, _, ff2 = w_gu.shape
    ff = ff2 // 2
    tm, tf = MOE_ROWS, min(MOE_FF_TILE, ff)
    nf = ff // tf
    n_blocks = block_expert.shape[0]
    assert n_tok < (1 << 16) and tm % nf == 0

    def live_expert(m, be, nu):
        return be[jnp.minimum(m, nu[0] - 1)]

    def live_tile(m, f, nu):
        return jnp.where(m < nu[0], f, nf - 1)

    def wmap(off):
        return lambda m, f, be, code, nu: (layer, live_expert(m, be, nu), 0, off + live_tile(m, f, nu))

    grid_spec = pltpu.PrefetchScalarGridSpec(
        num_scalar_prefetch=3,
        grid=(n_blocks, nf),
        in_specs=[pl.BlockSpec(memory_space=pl.ANY),
                  pl.BlockSpec((None, 1, d, tf), wmap(0)),
                  pl.BlockSpec((None, 1, d, tf), wmap(nf)),
                  pl.BlockSpec((None, 1, 1, tf), wmap(0)),
                  pl.BlockSpec((None, 1, 1, tf), wmap(nf)),
                  pl.BlockSpec((None, 1, tf, d),
                               lambda m, f, be, code, nu: (layer, live_expert(m, be, nu), live_tile(m, f, nu), 0)),
                  pl.BlockSpec((None, 1, 1, d), lambda m, f, be, code, nu: (layer, live_expert(m, be, nu), 0, 0)),
                  pl.BlockSpec((tm, 1), lambda m, f, be, code, nu: (jnp.minimum(m, nu[0] - 1), 0))],
        out_specs=pl.BlockSpec(memory_space=pl.ANY),
        scratch_shapes=[pltpu.VMEM((2, tm, d), f32), pltpu.VMEM((tm, d), bf16), pltpu.VMEM((2, tm, d), f32),
                        pltpu.SemaphoreType.DMA((2,)), pltpu.SemaphoreType.DMA(())])
    return pl.pallas_call(
        functools.partial(_moe_kernel, tm=tm, nf=nf, n_tok=n_tok),
        grid_spec=grid_spec,
        out_shape=jax.ShapeDtypeStruct((TOP_K * n_tok + tm, d), f32),
        compiler_params=_cparams("arbitrary", "arbitrary"),
        name="moe_experts",
    )(block_expert, slot_code, n_used, h2, w_gu, w_gu, b_gu, b_gu, w_down, b_down, slot_w)


def _combine_kernel(y0_ref, y1_ref, y2_ref, y3_ref, x_ref, mod_ref, o_ref):
    f = (y0_ref[...] + y1_ref[...]) + (y2_ref[...] + y3_ref[...])
    o_ref[0] = x_ref[0] + mod_ref[0, 5:6, :] * f


def _moe_combine(y4, x_all, mod_l, n_lat, n_rows_out):
    b, t_all, d = x_all.shape
    tb = SEQ_BLOCK
    nt_lat = n_lat // tb
    per_k = b * t_all // tb
    per_b = t_all // tb
    yspec = lambda k: pl.BlockSpec((tb, d), lambda bi, i: (k * per_k + bi * per_b + i, 0))
    return pl.pallas_call(
        _combine_kernel,
        grid=(b, n_rows_out // tb),
        in_specs=[yspec(0), yspec(1), yspec(2), yspec(3),
                  pl.BlockSpec((1, tb, d), lambda bi, i: (bi, i, 0)),
                  pl.BlockSpec((1, 6, d), lambda bi, i: (jnp.where(i >= nt_lat, 2, bi), 0, 0))],
        out_specs=pl.BlockSpec((1, tb, d), lambda bi, i: (bi, i, 0)),
        out_shape=jax.ShapeDtypeStruct((b, n_rows_out, d), f32),
        compiler_params=_cparams("parallel", "parallel"),
        name="moe_combine",
    )(y4, y4, y4, y4, x_all, mod_l)


def _route(logits, n_exp):
    n = logits.shape[0]
    tm = MOE_ROWS
    i32 = jnp.int32
    top_val, top_idx = lax.top_k(logits, TOP_K)
    gate = jax.nn.softmax(top_val, axis=-1)
    e_flat = top_idx.reshape(-1).astype(i32)
    order = jnp.argsort(e_flat).astype(i32)
    gate_sorted = gate.reshape(-1)[order]
    counts = jnp.sum((e_flat[:, None] == jnp.arange(n_exp, dtype=i32)[None, :]).astype(i32), axis=0)
    padded = (counts + tm - 1) // tm * tm
    start = jnp.cumsum(counts) - counts
    pad_end = jnp.cumsum(padded)
    pad_start = pad_end - padded
    n_blocks = -(-(n * TOP_K) // tm) + n_exp
    blk_start = jnp.arange(n_blocks, dtype=i32) * tm
    block_expert = jnp.minimum(jnp.sum((pad_end[None, :] <= blk_start[:, None]).astype(i32), axis=1), n_exp - 1)
    n_used = (pad_end[-1] // tm).astype(i32).reshape(1)
    slot = jnp.arange(n_blocks * tm, dtype=i32)
    e_slot = jnp.repeat(block_expert, tm)
    off = slot - pad_start[e_slot]
    valid = (off < counts[e_slot]) & (slot < pad_end[-1])
    j = jnp.clip(start[e_slot] + off, 0, n * TOP_K - 1)
    item = order[j]
    code = jnp.where(valid, (item // TOP_K) | ((item % TOP_K) << 16), (slot % tm) | (TOP_K << 16)).astype(i32)
    slot_w = jnp.where(valid, gate_sorted[j], 0.0).astype(f32)
    return block_expert.astype(i32), code, n_used, slot_w[:, None]


def _rope_perm():
    q = A_HEAD_DIM // 4
    one = np.concatenate([np.arange(0, q), np.arange(2 * q, 3 * q), np.arange(q, 2 * q), np.arange(3 * q, 4 * q)])
    return one


def _rope_tables(n_lat, n_ctx):
    rows = n_lat // GRID_W
    row = jnp.repeat(jnp.arange(rows, dtype=f32), GRID_W)
    col = jnp.tile(jnp.arange(GRID_W, dtype=f32), rows)
    n_freq = A_HEAD_DIM // 4
    inv = ROPE_BASE ** (-jnp.arange(n_freq, dtype=f32) / n_freq)
    ang = jnp.concatenate([row[:, None] * inv, col[:, None] * inv], axis=1)
    cos, sin = jnp.cos(ang), jnp.sin(ang)
    cos_t = jnp.tile(cos, (1, 4))
    sin_t = jnp.tile(jnp.concatenate([-sin, sin], axis=1), (1, 2))
    cos_t = jnp.concatenate([cos_t, jnp.ones((n_ctx, LANES), f32)], axis=0)
    sin_t = jnp.concatenate([sin_t, jnp.zeros((n_ctx, LANES), f32)], axis=0)
    return cos_t, sin_t


def _pad_cols(w, n):
    return jnp.pad(w, ((0, 0), (0, n - w.shape[1])))


def _moe_layer(h2, logits, x_all, mod_l, w_gu16, b_gu4, w_down16, b_down4, layer, n_lat, n_rows_out):
    b, t_all, d = x_all.shape
    n_exp = w_gu16.shape[1]
    block_expert, slot_code, n_used, slot_w = _route(logits.reshape(b * t_all, LANES)[:, :n_exp], n_exp)
    y4 = _moe_experts(h2.reshape(b * t_all, d), block_expert, slot_code, n_used, slot_w,
                      w_gu16, b_gu4, w_down16, b_down4, layer)
    return _moe_combine(y4, x_all, mod_l, n_lat, n_rows_out)


def kernel(x, c, ctx, c_ctx, w_mod, b_mod, norm_g, e_w_in, e_w_out, e_q_gain, e_k_gain, e_sinks, e_conv_w, e_a_log, e_dt_bias, e_dn_norm, o_w_in, o_gate_w2, o_gate_b, o_gla_norm, o_w_out, w_router, b_router, w_gu, b_gu, w_down, b_down):
    b, s, d = x.shape
    n_ctx = ctx.shape[1]
    depth = w_mod.shape[0]
    n_exp = w_router.shape[2]
    assert n_ctx == SEQ_BLOCK and s % SEQ_BLOCK == 0 and b == 2
    x_all = jnp.concatenate([x, ctx], axis=1)
    t_all = s + n_ctx

    c8 = jnp.concatenate([c, c_ctx[None, :], jnp.zeros((8 - b - 1, d), f32)], axis=0)
    mod = _mod_vectors(c8, w_mod, b_mod).reshape(depth, 8, 6, d)
    wr = jnp.pad(w_router, ((0, 0), (0, 0), (0, LANES - n_exp)))
    br = jnp.pad(b_router, ((0, 0), (0, LANES - n_exp))).reshape(depth, 1, LANES)
    cos_t, sin_t = _rope_tables(s, n_ctx)
    perm = _rope_perm()
    w_gu16, w_down16 = w_gu.astype(bf16), w_down.astype(bf16)
    b_gu4 = b_gu.reshape(depth, n_exp, 1, b_gu.shape[2])
    b_down4 = b_down.reshape(depth, n_exp, 1, d)
    aq = A_HEADS * A_HEAD_DIM
    akv = A_KV_HEADS * A_HEAD_DIM
    bw = B_HEADS * B_HEAD_DIM

    for layer in range(depth):
        i = layer // 2
        mod_l = mod[layer]
        last = layer == depth - 1
        if layer % 2 == 0:
            w = e_w_in[i]
            qcols = (np.arange(A_HEADS)[:, None] * A_HEAD_DIM + perm[None, :]).reshape(-1)
            kcols = aq + (np.arange(A_KV_HEADS)[:, None] * A_HEAD_DIM + perm[None, :]).reshape(-1)
            o_dn = aq + 2 * akv
            o_z = o_dn + 3 * bw
            o_g = o_z + bw
            w_cat = jnp.concatenate([w[:, qcols], w[:, o_z:o_g], w[:, o_dn:o_z], w[:, kcols],
                                     w[:, aq + akv:aq + 2 * akv], w[:, o_g:]], axis=1)
            n_pad = 5760
            w16 = _pad_cols(w_cat, n_pad).astype(bf16)
            proj = _input_projection(x_all, norm_g[layer, 0], mod_l, w16, s, 1152)
            gain = jnp.concatenate([jnp.tile(e_q_gain[i][perm], A_HEADS) * (A_HEAD_DIM ** -0.5),
                                    jnp.tile(e_k_gain[i][perm], A_KV_HEADS)]).reshape(1, aq + akv)
            q16, kt16, vt16 = _qk_prepare(proj, cos_t, sin_t, gain, (2 * bw + 3 * bw) // 512)
            a_all = _attention(e_sinks[i], q16, kt16, vt16, s)
            conv8 = jnp.pad(e_conv_w[i], ((0, 8 - CONV_W), (0, 0)))
            qkv = _dn_prepare(proj, conv8, s, 2)
            gate_block = (n_pad - LANES) // LANES
            gates_t = jnp.swapaxes(proj[:, :, n_pad - LANES:n_pad - LANES + 4 * B_HEADS], 1, 2)
            o_f = _dn_scan(qkv, proj, gates_t, e_a_log[i], e_dt_bias[i], None, s, gate_block, rev=False)
            o_mix = _dn_scan(qkv, proj, gates_t, e_a_log[i], e_dt_bias[i], o_f, s, gate_block, rev=True)
            gnorm = jnp.tile(e_dn_norm[i], B_HEADS).reshape(1, bw)
            x_all, h2, logits = _mixer_output(a_all, o_mix, proj, 1, gnorm, e_w_out[i].astype(bf16), x_all, mod_l,
                                              norm_g[layer, 1], wr[layer], br[layer], s, B_HEAD_DIM)
        else:
            w = o_w_in[i]
            n_pad = 6272
            proj = _input_projection(x_all, norm_g[layer, 0], mod_l, _pad_cols(w, n_pad).astype(bf16), s, 896)
            dk_all = o_gate_w2.shape[3]
            o_mix = None
            for dirn in range(2):
                w2pad = jnp.zeros((LANES, dk_all), f32).at[dirn * GATE_RANK:(dirn + 1) * GATE_RANK].set(
                    o_gate_w2[i, dirn])
                o_mix = _gla_scan(proj, w2pad, o_gate_b[i, dirn].reshape(1, dk_all), o_mix, s, rev=dirn == 1)
            dv = o_gla_norm.shape[1]
            gnorm = jnp.tile(o_gla_norm[i], C_HEADS).reshape(1, C_HEADS * dv)
            x_all, h2, logits = _mixer_output(None, o_mix, proj, 2, gnorm, o_w_out[i].astype(bf16), x_all, mod_l,
                                              norm_g[layer, 1], wr[layer], br[layer], s, dv)
        x_all = _moe_layer(h2, logits, x_all, mod_l, w_gu16, b_gu4, w_down16, b_down4, layer, s,
                           s if last else t_all)
    return x_all
```

```python
import functools

import numpy as np
import jax
import jax.numpy as jnp
from jax import lax
from jax.experimental import pallas as pl
from jax.experimental.pallas import tpu as pltpu

f32 = jnp.float32
bf16 = jnp.bfloat16

NORM_EPS = 1e-6
NEG_INF = -1e30
GRID_W = 64
ROPE_BASE = 10000.0
A_HEADS, A_KV_HEADS, A_HEAD_DIM, WINDOW = 16, 4, 64, 128
A_GROUP = A_HEADS // A_KV_HEADS
B_HEADS, B_HEAD_DIM, CONV_W = 8, 128, 5
CHUNK = 64
DN_NEWTON_STEPS = 2
C_HEADS = 4
GATE_RANK = 16
GATE_NORMALIZER = 16.0
TOP_K = 4
SWIGLU_ALPHA, SWIGLU_LIMIT = 1.702, 7.0

LANES = 128
SEQ_BLOCK = 256
VMEM_LIMIT = 56 * 1024 * 1024
MOE_ROWS = 512
MOE_FF_TILE = 1024


def _cparams(*sem):
    return pltpu.CompilerParams(dimension_semantics=sem, vmem_limit_bytes=VMEM_LIMIT)


def _split2(x):
    hi = x.astype(bf16)
    return hi, (x - hi.astype(f32)).astype(bf16)


def _split3(x):
    x1 = x.astype(bf16)
    r1 = x - x1.astype(f32)
    x2 = r1.astype(bf16)
    x3 = (r1 - x2.astype(f32)).astype(bf16)
    return x1, x2, x3


def _mm(a, b):
    return jnp.dot(a, b, preferred_element_type=f32)


def _mm_nt(a, b):
    return lax.dot_general(a, b, (((1,), (1,)), ((), ())), preferred_element_type=f32)


def _mm_lhs01(a01, x):
    x1, x2, x3 = _split3(x)
    return _mm(a01, x1) + _mm(a01, x2) + _mm(a01, x3)


def _mm_rhs01(x, b01):
    x1, x2, x3 = _split3(x)
    return _mm(x1, b01) + _mm(x2, b01) + _mm(x3, b01)


def _mm_x3(a, b):
    a1, a2 = _split2(a)
    b1, b2 = _split2(b)
    return _mm(a1, b1) + _mm(a1, b2) + _mm(a2, b1)


def _sigmoid(x):
    return 1.0 / (1.0 + jnp.exp(-x))


def _softplus(x):
    return jnp.maximum(x, 0.0) + jnp.log(1.0 + jnp.exp(-jnp.abs(x)))


def _mask01(m):
    return jnp.where(m, 1.0, 0.0).astype(bf16)


def _chunk_masks(n, rev):
    row = lax.broadcasted_iota(jnp.int32, (n, n), 0)
    col = lax.broadcasted_iota(jnp.int32, (n, n), 1)
    same = (row // CHUNK) == (col // CHUNK)
    if rev:
        return same, same & (col >= row), same & (col > row), row, col
    return same, same & (col <= row), same & (col < row), row, col


def _mod_kernel(c_ref, w_ref, b_ref, o_ref):
    c = c_ref[...]
    o_ref[0] = _mm_x3(c * _sigmoid(c), w_ref[0]) + b_ref[0]


def _mod_vectors(c8, w_mod, b_mod):
    depth, d, n = w_mod.shape
    tn = 1024
    return pl.pallas_call(
        _mod_kernel,
        grid=(depth, n // tn),
        in_specs=[pl.BlockSpec((8, d), lambda l, j: (0, 0)),
                  pl.BlockSpec((1, d, tn), lambda l, j: (l, 0, j)),
                  pl.BlockSpec((1, 1, tn), lambda l, j: (l, 0, j))],
        out_specs=pl.BlockSpec((1, 8, tn), lambda l, j: (l, 0, j)),
        out_shape=jax.ShapeDtypeStruct((depth, 8, n), f32),
        compiler_params=_cparams("parallel", "parallel"),
        name="mod_vectors",
    )(c8, w_mod, b_mod.reshape(depth, 1, n))


def _proj_kernel(x_ref, g_ref, mb_ref, mc_ref, w_ref, o_ref, h_ref, *, tm, n_lat, sub):
    i = pl.program_id(1)

    @pl.when(pl.program_id(2) == 0)
    def _():
        for r0 in range(0, tm, sub):
            x = x_ref[0, r0:r0 + sub, :]
            y = x * lax.rsqrt(jnp.mean(x * x, axis=-1, keepdims=True) + NORM_EPS) * g_ref[...]
            row = i * tm + r0 + lax.broadcasted_iota(jnp.int32, (sub, 1), 0)
            is_ctx = row >= n_lat
            shift = jnp.where(is_ctx, mc_ref[0, 0:1, :], mb_ref[0, 0:1, :])
            scale = jnp.where(is_ctx, mc_ref[0, 1:2, :], mb_ref[0, 1:2, :])
            h_ref[r0:r0 + sub, :] = (y * (1.0 + scale) + shift).astype(bf16)

    o_ref[0] = _mm(h_ref[...], w_ref[...])


def _input_projection(x_all, gain, mod_l, w, n_lat, tn):
    b, t_all, d = x_all.shape
    n = w.shape[1]
    tm = 1280 if t_all % 1280 == 0 else SEQ_BLOCK
    assert t_all % tm == 0 and n % tn == 0
    return pl.pallas_call(
        functools.partial(_proj_kernel, tm=tm, n_lat=n_lat, sub=128),
        grid=(b, t_all // tm, n // tn),
        in_specs=[pl.BlockSpec((1, tm, d), lambda bi, i, j: (bi, i, 0)),
                  pl.BlockSpec((1, d), lambda bi, i, j: (0, 0)),
                  pl.BlockSpec((1, 6, d), lambda bi, i, j: (bi, 0, 0)),
                  pl.BlockSpec((1, 6, d), lambda bi, i, j: (2, 0, 0)),
                  pl.BlockSpec((d, tn), lambda bi, i, j: (0, j))],
        out_specs=pl.BlockSpec((1, tm, tn), lambda bi, i, j: (bi, i, j)),
        out_shape=jax.ShapeDtypeStruct((b, t_all, n), f32),
        scratch_shapes=[pltpu.VMEM((tm, d), bf16)],
        compiler_params=_cparams("parallel", "parallel", "arbitrary"),
        name="input_projection",
    )(x_all, gain.reshape(1, d), mod_l, mod_l, w)


def _qkprep_kernel(q_ref, kv_ref, cos_ref, sin_ref, gain_ref, ones_ref, e_ref, qo_ref, ko_ref, vo_ref):
    cos = cos_ref[...]
    sin = sin_ref[...]
    lane = lax.broadcasted_iota(jnp.int32, cos.shape, 1)
    upper = (lane & 32) != 0
    ones = ones_ref[...]

    def norm_rope(x, gain):
        ms = _mm_rhs01(x * x, ones) * (1.0 / A_HEAD_DIM)
        y = x * lax.rsqrt(ms + NORM_EPS) * gain
        outs = []
        for c in range(2):
            yc = y[:, c * LANES:(c + 1) * LANES]
            partner = jnp.where(upper, pltpu.roll(yc, 32, 1), pltpu.roll(yc, 96, 1))
            outs.append(yc * cos + partner * sin)
        return jnp.concatenate(outs, axis=1)

    for c in range(4):
        sl = slice(c * 256, (c + 1) * 256)
        qo_ref[0, :, sl] = norm_rope(q_ref[0, :, sl], gain_ref[:, sl]).astype(bf16)
    k = norm_rope(kv_ref[0, :, 0:256], gain_ref[:, 1024:1280]).astype(bf16)
    ko_ref[0] = _mm(k, e_ref[...]).astype(bf16)
    vo_ref[0] = _mm(kv_ref[0, :, 256:512].astype(bf16), e_ref[...]).astype(bf16)


def _qk_prepare(proj, cos_t, sin_t, gain, kv_block):
    b, t_all, _ = proj.shape
    tt = SEQ_BLOCK
    hd = A_HEAD_DIM
    ones = np.kron(np.eye(4, dtype=np.float32), np.ones((hd, hd), np.float32))
    expand = np.kron(np.eye(4, dtype=np.float32), np.tile(np.eye(hd, dtype=np.float32), (1, A_GROUP)))
    out = jax.ShapeDtypeStruct((b, t_all, 1024), bf16)
    return pl.pallas_call(
        _qkprep_kernel,
        grid=(b, t_all // tt),
        in_specs=[pl.BlockSpec((1, tt, 1024), lambda bi, i: (bi, i, 0)),
                  pl.BlockSpec((1, tt, 512), lambda bi, i: (bi, i, kv_block)),
                  pl.BlockSpec((tt, LANES), lambda bi, i: (i, 0)),
                  pl.BlockSpec((tt, LANES), lambda bi, i: (i, 0)),
                  pl.BlockSpec((1, 1280), lambda bi, i: (0, 0)),
                  pl.BlockSpec((256, 256), lambda bi, i: (0, 0)),
                  pl.BlockSpec((256, 1024), lambda bi, i: (0, 0))],
        out_specs=[pl.BlockSpec((1, tt, 1024), lambda bi, i: (bi, i, 0))] * 3,
        out_shape=[out, out, out],
        compiler_params=_cparams("parallel", "parallel"),
        name="qk_prepare",
    )(proj, proj, cos_t, sin_t, gain, jnp.asarray(ones, bf16), jnp.asarray(expand, bf16))


def _attn_kernel(sink_ref, q_ref, kp_ref, kc_ref, kn_ref, kx_ref, vp_ref, vc_ref, vn_ref, vx_ref, o_ref, *,
                 nq_lat, n_ctx):
    n = pl.program_id(1)
    blk = WINDOW
    rows = A_GROUP * blk
    nband = 3 * blk
    ncol = nband + n_ctx
    rq = lax.broadcasted_iota(jnp.int32, (rows, ncol), 0) % blk
    col = lax.broadcasted_iota(jnp.int32, (rows, ncol), 1)
    rgrp = lax.broadcasted_iota(jnp.int32, (rows, 1), 0) // blk
    dist = col - rq
    is_lat = n < nq_lat
    lo = jnp.where(n > 0, 0, blk)
    hi = jnp.where(is_lat, jnp.where(n < nq_lat - 1, nband, 2 * blk), 0)
    allowed = (col >= nband) | ((dist >= 0) & (dist <= 2 * WINDOW) & (col >= lo) & (col < hi))
    lane_g = lax.broadcasted_iota(jnp.int32, (blk, A_GROUP * A_HEAD_DIM), 1) // A_HEAD_DIM
    for h in range(A_KV_HEADS):
        sl = slice(h * 256, (h + 1) * 256)
        qh = q_ref[0, :, sl]
        qs = jnp.concatenate([jnp.where(lane_g == g, qh, jnp.zeros_like(qh)) for g in range(A_GROUP)], axis=0)
        kcat = jnp.concatenate([kp_ref[0, :, sl], kc_ref[0, :, sl], kn_ref[0, :, sl], kx_ref[0, :, sl]], axis=0)
        vcat = jnp.concatenate([vp_ref[0, :, sl], vc_ref[0, :, sl], vn_ref[0, :, sl], vx_ref[0, :, sl]], axis=0)
        s = jnp.where(allowed, _mm_nt(qs, kcat), NEG_INF)
        sink = jnp.zeros((rows, 1), f32)
        for g in range(A_GROUP):
            sink = jnp.where(rgrp == g, sink_ref[h * A_GROUP + g], sink)
        m = jnp.maximum(jnp.max(s, axis=-1, keepdims=True), sink)
        p = jnp.exp(s - m)
        den = jnp.sum(p, axis=-1, keepdims=True) + jnp.exp(sink - m)
        o = _mm(p.astype(bf16), vcat) * (1.0 / den)
        out = jnp.zeros((blk, 256), f32)
        for g in range(A_GROUP):
            out = out + jnp.where(lane_g == g, o[g * blk:(g + 1) * blk, :], 0.0)
        o_ref[0, :, sl] = out.astype(bf16)


def _attention(sinks, q, kt, vt, n_lat):
    b, t_all, w = q.shape
    blk = WINDOW
    nq = t_all // blk
    n_ctx = t_all - n_lat
    ctx_blk = n_lat // n_ctx
    qspec = pl.BlockSpec((1, blk, w), lambda bi, i: (bi, i, 0))
    prev = pl.BlockSpec((1, blk, w), lambda bi, i: (bi, jnp.maximum(i - 1, 0), 0))
    nxt = pl.BlockSpec((1, blk, w), lambda bi, i: (bi, jnp.minimum(i + 1, nq - 1), 0))
    ctx = pl.BlockSpec((1, n_ctx, w), lambda bi, i: (bi, ctx_blk, 0))
    return pl.pallas_call(
        functools.partial(_attn_kernel, nq_lat=n_lat // blk, n_ctx=n_ctx),
        grid=(b, nq),
        in_specs=[pl.BlockSpec(memory_space=pltpu.SMEM), qspec, prev, qspec, nxt, ctx, prev, qspec, nxt, ctx],
        out_specs=qspec,
        out_shape=jax.ShapeDtypeStruct((b, t_all, w), bf16),
        compiler_params=_cparams("parallel", "parallel"),
        name="window_attention",
    )(sinks, q, kt, kt, kt, kt, vt, vt, vt, vt)


def _dnprep_kernel(x_ref, xp_ref, xn_ref, w_ref, o_ref, ext_ref, *, nt_lat, tt):
    i = pl.program_id(1)
    j = pl.program_id(2)
    keep_prev = jnp.where((i == 0) | (i == nt_lat), 0.0, 1.0)
    keep_next = jnp.where((i == nt_lat - 1) | (i == nt_lat), 0.0, 1.0)
    ext_ref[0:8, :] = xp_ref[0] * keep_prev
    ext_ref[8:8 + tt, :] = x_ref[0]
    ext_ref[8 + tt:16 + tt, :] = xn_ref[0] * keep_next
    half = CONV_W // 2
    acc = jnp.zeros((tt, x_ref.shape[2]), f32)
    for tap in range(CONV_W):
        acc = acc + ext_ref[pl.ds(8 - half + tap, tt), :] * w_ref[tap:tap + 1, :]
    y = acc * _sigmoid(acc)
    q_scale = jnp.where(j == 0, B_HEAD_DIM ** -0.5, 1.0)
    for h in range(x_ref.shape[2] // B_HEAD_DIM):
        sl = slice(h * B_HEAD_DIM, (h + 1) * B_HEAD_DIM)
        yh = y[:, sl]
        rn = lax.rsqrt(jnp.sum(yh * yh, axis=-1, keepdims=True) + 1e-6) * q_scale
        o_ref[0, :, sl] = yh * jnp.where(j == 2, 1.0, rn)


def _dn_prepare(proj, conv_w8, n_lat, col_block):
    b, t_all, _ = proj.shape
    tt = SEQ_BLOCK
    bw = B_HEADS * B_HEAD_DIM
    r8 = tt // 8
    return pl.pallas_call(
        functools.partial(_dnprep_kernel, nt_lat=n_lat // tt, tt=tt),
        grid=(b, t_all // tt, 3),
        in_specs=[pl.BlockSpec((1, tt, bw), lambda bi, i, j: (bi, i, col_block + j)),
                  pl.BlockSpec((1, 8, bw), lambda bi, i, j: (bi, jnp.maximum(i * r8 - 1, 0), col_block + j)),
                  pl.BlockSpec((1, 8, bw), lambda bi, i, j: (bi, jnp.minimum((i + 1) * r8, t_all // 8 - 1),
                                                             col_block + j)),
                  pl.BlockSpec((8, bw), lambda bi, i, j: (0, j))],
        out_specs=pl.BlockSpec((1, tt, bw), lambda bi, i, j: (bi, i, j)),
        out_shape=jax.ShapeDtypeStruct((b, t_all, 3 * bw), f32),
        scratch_shapes=[pltpu.VMEM((tt + 16, bw), f32)],
        compiler_params=_cparams("parallel", "parallel", "parallel"),
        name="deltanet_prepare",
    )(proj, proj, proj, conv_w8)


def _dnscan_kernel(*refs, rev, add_prev):
    if add_prev:
        (q_ref, k_ref, v_ref, gc_ref, gt_ref, alr_ref, dtr_ref, alc_ref, dtc_ref, op_ref, o_ref,
         s_ref, u_scr, w_scr, qd_scr, kdt_scr, at_scr, vn_scr, egl_scr) = refs
    else:
        (q_ref, k_ref, v_ref, gc_ref, gt_ref, alr_ref, dtr_ref, alc_ref, dtc_ref, o_ref,
         s_ref, u_scr, w_scr, qd_scr, kdt_scr, at_scr, vn_scr, egl_scr) = refs
        op_ref = None
    tt = q_ref.shape[1]
    hd = B_HEAD_DIM
    d = 1 if rev else 0

    @pl.when(pl.program_id(1) == 0)
    def _():
        s_ref[...] = jnp.zeros(s_ref.shape, f32)

    same, incl, strict, row, col = _chunk_masks(tt, rev)
    tri = _mask01(incl)
    tri_t = _mask01(same & ((row >= col) if rev else (row <= col)))
    ones_bd = _mask01(same)
    eye = jnp.where(row == col, 1.0, 0.0)

    gates = gc_ref[0]
    g_all = -jnp.exp(alr_ref[...]) * _softplus(gates + dtr_ref[...])
    beta_all = _sigmoid(gates)
    gcum = _mm_lhs01(tri, g_all)
    gsum = _mm_lhs01(ones_bd, g_all)
    egl_scr[...] = jnp.exp(gsum)
    g_row = -jnp.exp(alc_ref[...]) * _softplus(gt_ref[0] + dtc_ref[...])
    gcum_r = _mm_rhs01(g_row, tri_t)

    ms, ps, rhss, m0s = [], [], [], []
    for h in range(B_HEADS):
        li = d * B_HEADS + h
        sl = slice(h * hd, (h + 1) * hd)
        qh, kh, vh = q_ref[0, :, sl], k_ref[0, :, sl], v_ref[0, :, sl]
        gc = gcum[:, li:li + 1]
        gs = gsum[:, li:li + 1]
        beta = beta_all[:, 2 * B_HEADS + li:2 * B_HEADS + li + 1]
        dec = jnp.where(incl, jnp.exp(jnp.where(incl, gc - gcum_r[li:li + 1, :], 0.0)), 0.0)
        kb = kh * beta
        kh16 = kh.astype(bf16)
        m = -jnp.where(strict, _mm_nt(kb.astype(bf16), kh16) * dec, 0.0)
        ms.append(m)
        m0s.append(_split2(m))
        ps.append(eye + m)
        egc = jnp.exp(gc)
        rhss.append(_split2(jnp.concatenate([vh * beta, kb * egc], axis=1)))
        at_scr[h] = jnp.where(incl, _mm_nt(qh.astype(bf16), kh16) * dec, 0.0)
        qd_scr[h] = qh * egc
        kdt_scr[h] = (kh * jnp.exp(gs - gc)).T
    for _ in range(5):
        for h in range(B_HEADS):
            m16 = ms[h].astype(bf16)
            ms[h] = _mm(m16, m16)
        for h in range(B_HEADS):
            ps[h] = ps[h] + _mm(ps[h].astype(bf16), ms[h].astype(bf16))
    for _ in range(DN_NEWTON_STEPS):
        xs = [_split2(ps[h]) for h in range(B_HEADS)]
        prods = [_mm(m0s[h][0], xs[h][0]) + _mm(m0s[h][0], xs[h][1]) + _mm(m0s[h][1], xs[h][0])
                 for h in range(B_HEADS)]
        resids = [((eye - ps[h]) + prods[h]).astype(bf16) for h in range(B_HEADS)]
        for h in range(B_HEADS):
            ps[h] = ps[h] + _mm(xs[h][0], resids[h])
    for h in range(B_HEADS):
        p_hi, p_lo = _split2(ps[h])
        r_hi, r_lo = rhss[h]
        sol = _mm(p_hi, r_hi) + _mm(p_hi, r_lo) + _mm(p_lo, r_hi)
        u_scr[h] = sol[:, :hd]
        vn_scr[h] = sol[:, :hd]
        w_scr[h] = sol[:, hd:]

    colk = lax.broadcasted_iota(jnp.int32, (hd, tt), 1) // CHUNK
    nchunk = tt // CHUNK
    for c in (range(nchunk - 1, -1, -1) if rev else range(nchunk)):
        rs = slice(c * CHUNK, (c + 1) * CHUNK)
        egl_row = egl_scr[c * CHUNK:c * CHUNK + 1, :]
        heads = range(B_HEADS)
        s_all = [s_ref[h] for h in heads]
        s16 = [s.astype(bf16) for s in s_all]
        for h in heads:
            vn_scr[h, rs, :] = u_scr[h, rs, :] - _mm(w_scr[h, rs, :].astype(bf16), s16[h])
        vn16 = [vn_scr[h].astype(bf16) for h in heads]
        for h in heads:
            sl = slice(h * hd, (h + 1) * hd)
            o = _mm(qd_scr[h, rs, :].astype(bf16), s16[h]) + _mm(at_scr[h, rs, :].astype(bf16), vn16[h])
            if op_ref is not None:
                o = o + op_ref[0, rs, sl]
            o_ref[0, rs, sl] = o
        for h in heads:
            li = d * B_HEADS + h
            kdt = jnp.where(colk == c, kdt_scr[h], 0.0).astype(bf16)
            s_ref[h] = s_all[h] * egl_row[:, li:li + 1] + _mm(kdt, vn16[h])


def _seq_block_index(i, nt_lat, rev):
    lat = (nt_lat - i) if rev else (i - 1)
    return jnp.where(i == 0, nt_lat, lat)


def _dn_scan(qkv, proj, gates_t, alog, dtb, o_prev, n_lat, gate_block, rev):
    b, t_all, _ = qkv.shape
    tt = SEQ_BLOCK
    nt_lat = n_lat // tt
    bw = B_HEADS * B_HEAD_DIM
    hd = B_HEAD_DIM
    blk = lambda bi, i: _seq_block_index(i, nt_lat, rev)
    pad = jnp.zeros((LANES - 2 * B_HEADS,), f32)
    alr = jnp.concatenate([alog.reshape(-1), pad]).reshape(1, LANES)
    dtr = jnp.concatenate([dtb.reshape(-1), pad]).reshape(1, LANES)
    alc = jnp.concatenate([alog.reshape(-1), pad[:2 * B_HEADS]]).reshape(4 * B_HEADS, 1)
    dtc = jnp.concatenate([dtb.reshape(-1), pad[:2 * B_HEADS]]).reshape(4 * B_HEADS, 1)
    small = lambda shape: pl.BlockSpec(shape, lambda bi, i: (0, 0))
    in_specs = [pl.BlockSpec((1, tt, bw), lambda bi, i: (bi, blk(bi, i), 0)),
                pl.BlockSpec((1, tt, bw), lambda bi, i: (bi, blk(bi, i), 1)),
                pl.BlockSpec((1, tt, bw), lambda bi, i: (bi, blk(bi, i), 2)),
                pl.BlockSpec((1, tt, LANES), lambda bi, i: (bi, blk(bi, i), gate_block)),
                pl.BlockSpec((1, 4 * B_HEADS, tt), lambda bi, i: (bi, 0, blk(bi, i))),
                small((1, LANES)), small((1, LANES)), small((4 * B_HEADS, 1)), small((4 * B_HEADS, 1))]
    args = [qkv, qkv, qkv, proj, gates_t, alr, dtr, alc, dtc]
    if o_prev is not None:
        in_specs.append(pl.BlockSpec((1, tt, bw), lambda bi, i: (bi, blk(bi, i), 0)))
        args.append(o_prev)
    head = lambda *s: pltpu.VMEM((B_HEADS,) + s, f32)
    return pl.pallas_call(
        functools.partial(_dnscan_kernel, rev=rev, add_prev=o_prev is not None),
        grid=(b, nt_lat + 1),
        in_specs=in_specs,
        out_specs=pl.BlockSpec((1, tt, bw), lambda bi, i: (bi, blk(bi, i), 0)),
        out_shape=jax.ShapeDtypeStruct((b, t_all, bw), f32),
        scratch_shapes=[head(hd, hd), head(tt, hd), head(tt, hd), head(tt, hd), head(hd, tt), head(tt, tt),
                        head(tt, hd), pltpu.VMEM((tt, LANES), f32)],
        compiler_params=_cparams("parallel", "arbitrary"),
        name="deltanet_scan_bwd" if rev else "deltanet_scan_fwd",
    )(*args)


def _glascan_kernel(*refs, rev, add_prev):
    if add_prev:
        q_ref, k_ref, v_ref, r_ref, w2_ref, gb_ref, op_ref, o_ref, s_ref = refs
    else:
        q_ref, k_ref, v_ref, r_ref, w2_ref, gb_ref, o_ref, s_ref = refs
        op_ref = None
    tt = q_ref.shape[1]
    dk = q_ref.shape[2] // C_HEADS
    dv = v_ref.shape[2] // C_HEADS
    heads = range(C_HEADS)

    @pl.when(pl.program_id(1) == 0)
    def _():
        s_ref[...] = jnp.zeros(s_ref.shape, f32)

    same, incl, _, _, _ = _chunk_masks(tt, rev)
    tri = _mask01(incl)
    ones_bd = _mask01(same)
    z = _mm_x3(r_ref[0], w2_ref[...]) + gb_ref[...]
    gk = (jnp.minimum(z, 0.0) - jnp.log(1.0 + jnp.exp(-jnp.abs(z)))) * (1.0 / GATE_NORMALIZER)
    bcum = _mm_lhs01(tri, gk)
    bsum = _mm_lhs01(ones_bd, gk)
    nchunk = tt // CHUNK
    mid = (CHUNK - 1 - CHUNK // 2) if rev else CHUNK // 2
    bmid = jnp.concatenate(
        [jnp.broadcast_to(bcum[c * CHUNK + mid:c * CHUNK + mid + 1, :], (CHUNK, bcum.shape[1]))
         for c in range(nchunk)], axis=0)
    q = q_ref[0] * (dk ** -0.5)
    k = k_ref[0]
    v16 = v_ref[0].astype(bf16)
    qe = (q * jnp.exp(bcum - bmid)).astype(bf16)
    ke = (k * jnp.exp(bmid - bcum)).astype(bf16)
    q_dec = (q * jnp.exp(bcum)).astype(bf16)
    kd = k * jnp.exp(bsum - bcum)
    dec = jnp.exp(bsum)
    ks = [slice(h * dk, (h + 1) * dk) for h in heads]
    vs = [slice(h * dv, (h + 1) * dv) for h in heads]
    attn = [jnp.where(incl, _mm_nt(qe[:, ks[h]], ke[:, ks[h]]), 0.0).astype(bf16) for h in heads]
    o_intra = [_mm(attn[h], v16[:, vs[h]]) for h in heads]
    kd_t = [kd[:, ks[h]].T for h in heads]
    dec_t = [dec[:, ks[h]].T for h in heads]
    colk = lax.broadcasted_iota(jnp.int32, (dk, tt), 1) // CHUNK
    for c in (range(nchunk - 1, -1, -1) if rev else range(nchunk)):
        rs = slice(c * CHUNK, (c + 1) * CHUNK)
        s_all = [s_ref[h] for h in heads]
        for h in heads:
            o = o_intra[h][rs, :] + _mm(q_dec[rs, ks[h]], s_all[h].astype(bf16))
            if op_ref is not None:
                o = o + op_ref[0, rs, vs[h]]
            o_ref[0, rs, vs[h]] = o
        for h in heads:
            kdt = jnp.where(colk == c, kd_t[h], 0.0).astype(bf16)
            s_ref[h] = s_all[h] * dec_t[h][:, c * CHUNK:c * CHUNK + 1] + _mm(kdt, v16[:, vs[h]])


def _gla_scan(proj, w2pad, gate_b, o_prev, n_lat, rev):
    b, t_all, _ = proj.shape
    tt = SEQ_BLOCK
    nt_lat = n_lat // tt
    dk_all = w2pad.shape[1]
    dk = dk_all // C_HEADS
    dv_all = 2 * dk_all
    blk = lambda i: _seq_block_index(i, nt_lat, rev)
    in_specs = [pl.BlockSpec((1, tt, dk_all), lambda bi, i: (bi, blk(i), 0)),
                pl.BlockSpec((1, tt, dk_all), lambda bi, i: (bi, blk(i), 1)),
                pl.BlockSpec((1, tt, dv_all), lambda bi, i: (bi, blk(i), 1)),
                pl.BlockSpec((1, tt, LANES), lambda bi, i: (bi, blk(i), (2 * dk_all + 2 * dv_all) // LANES)),
                pl.BlockSpec((LANES, dk_all), lambda bi, i: (0, 0)),
                pl.BlockSpec((1, dk_all), lambda bi, i: (0, 0))]
    args = [proj, proj, proj, proj, w2pad, gate_b]
    if o_prev is not None:
        in_specs.append(pl.BlockSpec((1, tt, dv_all), lambda bi, i: (bi, blk(i), 0)))
        args.append(o_prev)
    return pl.pallas_call(
        functools.partial(_glascan_kernel, rev=rev, add_prev=o_prev is not None),
        grid=(b, nt_lat + 1),
        in_specs=in_specs,
        out_specs=pl.BlockSpec((1, tt, dv_all), lambda bi, i: (bi, blk(i), 0)),
        out_shape=jax.ShapeDtypeStruct((b, t_all, dv_all), f32),
        scratch_shapes=[pltpu.VMEM((C_HEADS, dk, 2 * dk), f32)],
        compiler_params=_cparams("parallel", "arbitrary"),
        name="gla_scan_bwd" if rev else "gla_scan_fwd",
    )(*args)


def _out_kernel(*refs, group, has_a):
    if has_a:
        a_ref, o_ref, z_ref, gn_ref, w_ref, x_ref, mod_ref, g2_ref, wr_ref, br_ref, xo_ref, ho_ref, lo_ref = refs
    else:
        o_ref, z_ref, gn_ref, w_ref, x_ref, mod_ref, g2_ref, wr_ref, br_ref, xo_ref, ho_ref, lo_ref = refs
    width = o_ref.shape[2]
    z = z_ref[0]
    gate = z * _sigmoid(z)
    parts = []
    for g0 in range(0, width, group):
        og = o_ref[0, :, g0:g0 + group]
        parts.append(og * lax.rsqrt(jnp.mean(og * og, axis=-1, keepdims=True) + NORM_EPS))
    on = (jnp.concatenate(parts, axis=1) * gn_ref[...] * gate).astype(bf16)
    if has_a:
        na = a_ref.shape[2]
        y = _mm(a_ref[0], w_ref[0:na, :]) + _mm(on, w_ref[na:na + width, :])
    else:
        y = _mm(on, w_ref[...])
    xn = x_ref[0] + mod_ref[0, 2:3, :] * y
    xo_ref[0] = xn
    hn = xn * lax.rsqrt(jnp.mean(xn * xn, axis=-1, keepdims=True) + NORM_EPS) * g2_ref[...]
    h2 = hn * (1.0 + mod_ref[0, 4:5, :]) + mod_ref[0, 3:4, :]
    ho_ref[0] = h2
    lo_ref[0] = _mm_x3(h2, wr_ref[...]) + br_ref[...]


def _mixer_output(a, o, proj, z_block, gnorm, w_out, x_all, mod_l, g2, wr, br, n_lat, group):
    b, t_all, d = x_all.shape
    tm = SEQ_BLOCK
    nt_lat = n_lat // tm
    width = o.shape[2]
    row = lambda w: pl.BlockSpec((1, tm, w), lambda bi, i: (bi, i, 0))
    const = lambda shape: pl.BlockSpec(shape, lambda bi, i: (0,) * len(shape))
    in_specs, args = [], []
    if a is not None:
        in_specs.append(row(a.shape[2]))
        args.append(a)
    in_specs += [row(width),
                 pl.BlockSpec((1, tm, width), lambda bi, i: (bi, i, z_block)),
                 const((1, width)), const(w_out.shape), row(d),
                 pl.BlockSpec((1, 6, d), lambda bi, i: (jnp.where(i >= nt_lat, 2, bi), 0, 0)),
                 const((1, d)), const(wr.shape), const((1, LANES))]
    args += [o, proj, gnorm, w_out, x_all, mod_l, g2.reshape(1, d), wr, br]
    return pl.pallas_call(
        functools.partial(_out_kernel, group=group, has_a=a is not None),
        grid=(b, t_all // tm),
        in_specs=in_specs,
        out_specs=[row(d), row(d), row(LANES)],
        out_shape=[jax.ShapeDtypeStruct((b, t_all, d), f32), jax.ShapeDtypeStruct((b, t_all, d), f32),
                   jax.ShapeDtypeStruct((b, t_all, LANES), f32)],
        compiler_params=_cparams("parallel", "parallel"),
        name="mixer_output",
    )(*args)


def _moe_kernel(be_ref, code_ref, nused_ref, h_hbm, wg_ref, wl_ref, bg_ref, bl_ref, wd_ref, bd_ref, sw_ref, y_hbm,
                xbuf, x16, acc, gsem, ssem, *, tm, nf, n_tok):
    m = pl.program_id(0)
    f = pl.program_id(1)
    nu = nused_ref[0]
    active = m < nu
    cur = m % 2
    oth = 1 - cur
    rows = tm // nf

    def gather_start(blk, buf, r):
        tok = code_ref[blk * tm + r] & 0xFFFF
        pltpu.make_async_copy(h_hbm.at[pl.ds(tok, 1)], xbuf.at[buf, pl.ds(r, 1)], gsem.at[buf]).start()

    def gather_wait(buf):
        pltpu.make_async_copy(h_hbm.at[pl.ds(0, tm)], xbuf.at[buf], gsem.at[buf]).wait()

    def scatter_start(blk, buf, r, real):
        code = code_ref[blk * tm + r]
        dst = ((code >> 16) & 7) * n_tok + (code & 0xFFFF)
        dst = jnp.where(real, dst, TOP_K * n_tok + r)
        pltpu.make_async_copy(acc.at[buf, pl.ds(r, 1)], y_hbm.at[pl.ds(dst, 1)], ssem).start()

    def scatter_wait():
        pltpu.make_async_copy(acc.at[0], y_hbm.at[pl.ds(0, tm)], ssem).wait()

    @pl.when(active & (f == 0))
    def _():
        @pl.when(m == 0)
        def _():
            def issue(r, carry):
                gather_start(0, 0, r)
                return carry
            lax.fori_loop(0, tm, issue, 0)
            acc[1] = jnp.zeros(acc.shape[1:], f32)

        gather_wait(cur)

        @pl.when(m > 0)
        def _():
            scatter_wait()

        x16[...] = xbuf[cur].astype(bf16)

    nxt = jnp.minimum(m + 1, pl.num_programs(0) - 1)
    prv = jnp.maximum(m - 1, 0)
    for fs in range(nf):
        @pl.when(active & (f == fs))
        def _(fs=fs):
            for r in range(fs * rows, (fs + 1) * rows):
                gather_start(nxt, oth, r)
                scatter_start(prv, oth, r, m > 0)
            x = x16[...]
            glu = jnp.minimum(_mm(x, wg_ref[0]) + bg_ref[0], SWIGLU_LIMIT)
            lin = jnp.clip(_mm(x, wl_ref[0]) + bl_ref[0], -SWIGLU_LIMIT, SWIGLU_LIMIT)
            act = glu * _sigmoid(SWIGLU_ALPHA * glu) * (lin + 1.0)
            y = _mm(act.astype(bf16), wd_ref[0])
            if fs > 0:
                y = acc[cur] + y
            if fs == nf - 1:
                y = (y + bd_ref[0]) * sw_ref[...]
            acc[cur] = y

    @pl.when((m == nu - 1) & (f == nf - 1))
    def _():
        scatter_wait()

        def issue(r, carry):
            scatter_start(m, cur, r, True)
            return carry
        lax.fori_loop(0, tm, issue, 0)
        scatter_wait()
        gather_wait(oth)


def _moe_experts(h2, block_expert, slot_code, n_used, slot_w, w_gu, b_gu, w_down, b_down, layer):
    n_tok, d = h2.shape
    _, n_exp, _, ff2 = w_gu.shape
    ff = ff2 // 2
    tm, tf = MOE_ROWS, min(MOE_FF_TILE, ff)
    nf = ff // tf
    n_blocks = block_expert.shape[0]
    assert n_tok < (1 << 16) and tm % nf == 0

    def live_expert(m, be, nu):
        return be[jnp.minimum(m, nu[0] - 1)]

    def live_tile(m, f, nu):
        return jnp.where(m < nu[0], f, nf - 1)

    def wmap(off):
        return lambda m, f, be, code, nu: (layer, live_expert(m, be, nu), 0, off + live_tile(m, f, nu))

    grid_spec = pltpu.PrefetchScalarGridSpec(
        num_scalar_prefetch=3,
        grid=(n_blocks, nf),
        in_specs=[pl.BlockSpec(memory_space=pl.ANY),
                  pl.BlockSpec((None, 1, d, tf), wmap(0)),
                  pl.BlockSpec((None, 1, d, tf), wmap(nf)),
                  pl.BlockSpec((None, 1, 1, tf), wmap(0)),
                  pl.BlockSpec((None, 1, 1, tf), wmap(nf)),
                  pl.BlockSpec((None, 1, tf, d),
                               lambda m, f, be, code, nu: (layer, live_expert(m, be, nu), live_tile(m, f, nu), 0)),
                  pl.BlockSpec((None, 1, 1, d), lambda m, f, be, code, nu: (layer, live_expert(m, be, nu), 0, 0)),
                  pl.BlockSpec((tm, 1), lambda m, f, be, code, nu: (jnp.minimum(m, nu[0] - 1), 0))],
        out_specs=pl.BlockSpec(memory_space=pl.ANY),
        scratch_shapes=[pltpu.VMEM((2, tm, d), f32), pltpu.VMEM((tm, d), bf16), pltpu.VMEM((2, tm, d), f32),
                        pltpu.SemaphoreType.DMA((2,)), pltpu.SemaphoreType.DMA(())])
    return pl.pallas_call(
        functools.partial(_moe_kernel, tm=tm, nf=nf, n_tok=n_tok),
        grid_spec=grid_spec,
        out_shape=jax.ShapeDtypeStruct((TOP_K * n_tok + tm, d), f32),
        compiler_params=_cparams("arbitrary", "arbitrary"),
        name="moe_experts",
    )(block_expert, slot_code, n_used, h2, w_gu, w_gu, b_gu, b_gu, w_down, b_down, slot_w)


def _combine_kernel(y0_ref, y1_ref, y2_ref, y3_ref, x_ref, mod_ref, o_ref):
    f = (y0_ref[...] + y1_ref[...]) + (y2_ref[...] + y3_ref[...])
    o_ref[0] = x_ref[0] + mod_ref[0, 5:6, :] * f


def _moe_combine(y4, x_all, mod_l, n_lat, n_rows_out):
    b, t_all, d = x_all.shape
    tb = SEQ_BLOCK
    nt_lat = n_lat // tb
    per_k = b * t_all // tb
    per_b = t_all // tb
    yspec = lambda k: pl.BlockSpec((tb, d), lambda bi, i: (k * per_k + bi * per_b + i, 0))
    return pl.pallas_call(
        _combine_kernel,
        grid=(b, n_rows_out // tb),
        in_specs=[yspec(0), yspec(1), yspec(2), yspec(3),
                  pl.BlockSpec((1, tb, d), lambda bi, i: (bi, i, 0)),
                  pl.BlockSpec((1, 6, d), lambda bi, i: (jnp.where(i >= nt_lat, 2, bi), 0, 0))],
        out_specs=pl.BlockSpec((1, tb, d), lambda bi, i: (bi, i, 0)),
        out_shape=jax.ShapeDtypeStruct((b, n_rows_out, d), f32),
        compiler_params=_cparams("parallel", "parallel"),
        name="moe_combine",
    )(y4, y4, y4, y4, x_all, mod_l)


def _route(logits, n_exp):
    n = logits.shape[0]
    tm = MOE_ROWS
    i32 = jnp.int32
    top_val, top_idx = lax.top_k(logits, TOP_K)
    gate = jax.nn.softmax(top_val, axis=-1)
    e_flat = top_idx.reshape(-1).astype(i32)
    order = jnp.argsort(e_flat).astype(i32)
    gate_sorted = gate.reshape(-1)[order]
    counts = jnp.sum((e_flat[None, :] == jnp.arange(n_exp, dtype=i32)[:, None]).astype(i32), axis=1)
    padded = (counts + tm - 1) // tm * tm
    start = jnp.cumsum(counts) - counts
    pad_end = jnp.cumsum(padded)
    pad_start = pad_end - padded
    n_blocks = -(-(n * TOP_K) // tm) + n_exp
    blk_start = jnp.arange(n_blocks, dtype=i32) * tm
    block_expert = jnp.minimum(jnp.sum((pad_end[None, :] <= blk_start[:, None]).astype(i32), axis=1), n_exp - 1)
    n_used = (pad_end[-1] // tm).astype(i32).reshape(1)
    slot = jnp.arange(n_blocks * tm, dtype=i32)
    e_slot = jnp.repeat(block_expert, tm)
    off = slot - pad_start[e_slot]
    valid = (off < counts[e_slot]) & (slot < pad_end[-1])
    j = jnp.clip(start[e_slot] + off, 0, n * TOP_K - 1)
    item = order[j]
    code = jnp.where(valid, (item // TOP_K) | ((item % TOP_K) << 16), (slot % tm) | (TOP_K << 16)).astype(i32)
    slot_w = jnp.where(valid, gate_sorted[j], 0.0).astype(f32)
    return block_expert.astype(i32), code, n_used, slot_w[:, None]


def _rope_perm():
    q = A_HEAD_DIM // 4
    one = np.concatenate([np.arange(0, q), np.arange(2 * q, 3 * q), np.arange(q, 2 * q), np.arange(3 * q, 4 * q)])
    return one


def _rope_tables(n_lat, n_ctx):
    rows = n_lat // GRID_W
    row = jnp.repeat(jnp.arange(rows, dtype=f32), GRID_W)
    col = jnp.tile(jnp.arange(GRID_W, dtype=f32), rows)
    n_freq = A_HEAD_DIM // 4
    inv = ROPE_BASE ** (-jnp.arange(n_freq, dtype=f32) / n_freq)
    ang = jnp.concatenate([row[:, None] * inv, col[:, None] * inv], axis=1)
    cos, sin = jnp.cos(ang), jnp.sin(ang)
    cos_t = jnp.tile(cos, (1, 4))
    sin_t = jnp.tile(jnp.concatenate([-sin, sin], axis=1), (1, 2))
    cos_t = jnp.concatenate([cos_t, jnp.ones((n_ctx, LANES), f32)], axis=0)
    sin_t = jnp.concatenate([sin_t, jnp.zeros((n_ctx, LANES), f32)], axis=0)
    return cos_t, sin_t


def _pad_cols(w, n):
    return jnp.pad(w, ((0, 0), (0, n - w.shape[1])))


def _moe_layer(h2, logits, x_all, mod_l, w_gu16, b_gu4, w_down16, b_down4, layer, n_lat, n_rows_out):
    b, t_all, d = x_all.shape
    n_exp = w_gu16.shape[1]
    block_expert, slot_code, n_used, slot_w = _route(logits.reshape(b * t_all, LANES)[:, :n_exp], n_exp)
    y4 = _moe_experts(h2.reshape(b * t_all, d), block_expert, slot_code, n_used, slot_w,
                      w_gu16, b_gu4, w_down16, b_down4, layer)
    return _moe_combine(y4, x_all, mod_l, n_lat, n_rows_out)


def kernel(x, c, ctx, c_ctx, w_mod, b_mod, norm_g, e_w_in, e_w_out, e_q_gain, e_k_gain, e_sinks, e_conv_w, e_a_log, e_dt_bias, e_dn_norm, o_w_in, o_gate_w2, o_gate_b, o_gla_norm, o_w_out, w_router, b_router, w_gu, b_gu, w_down, b_down):
    b, s, d = x.shape
    n_ctx = ctx.shape[1]
    depth = w_mod.shape[0]
    n_exp = w_router.shape[2]
    assert n_ctx == SEQ_BLOCK and s % SEQ_BLOCK == 0 and b == 2
    x_all = jnp.concatenate([x, ctx], axis=1)
    t_all = s + n_ctx

    c8 = jnp.concatenate([c, c_ctx[None, :], jnp.zeros((8 - b - 1, d), f32)], axis=0)
    mod = _mod_vectors(c8, w_mod, b_mod).reshape(depth, 8, 6, d)
    wr = jnp.pad(w_router, ((0, 0), (0, 0), (0, LANES - n_exp)))
    br = jnp.pad(b_router, ((0, 0), (0, LANES - n_exp))).reshape(depth, 1, LANES)
    cos_t, sin_t = _rope_tables(s, n_ctx)
    perm = _rope_perm()
    w_gu16, w_down16 = w_gu.astype(bf16), w_down.astype(bf16)
    b_gu4 = b_gu.reshape(depth, n_exp, 1, b_gu.shape[2])
    b_down4 = b_down.reshape(depth, n_exp, 1, d)
    aq = A_HEADS * A_HEAD_DIM
    akv = A_KV_HEADS * A_HEAD_DIM
    bw = B_HEADS * B_HEAD_DIM

    for layer in range(depth):
        i = layer // 2
        mod_l = mod[layer]
        last = layer == depth - 1
        if layer % 2 == 0:
            w = e_w_in[i]
            qcols = (np.arange(A_HEADS)[:, None] * A_HEAD_DIM + perm[None, :]).reshape(-1)
            kcols = aq + (np.arange(A_KV_HEADS)[:, None] * A_HEAD_DIM + perm[None, :]).reshape(-1)
            o_dn = aq + 2 * akv
            o_z = o_dn + 3 * bw
            o_g = o_z + bw
            w_cat = jnp.concatenate([w[:, qcols], w[:, o_z:o_g], w[:, o_dn:o_z], w[:, kcols],
                                     w[:, aq + akv:aq + 2 * akv], w[:, o_g:]], axis=1)
            n_pad = 5760
            w16 = _pad_cols(w_cat, n_pad).astype(bf16)
            proj = _input_projection(x_all, norm_g[layer, 0], mod_l, w16, s, 1152)
            gain = jnp.concatenate([jnp.tile(e_q_gain[i][perm], A_HEADS) * (A_HEAD_DIM ** -0.5),
                                    jnp.tile(e_k_gain[i][perm], A_KV_HEADS)]).reshape(1, aq + akv)
            q16, kt16, vt16 = _qk_prepare(proj, cos_t, sin_t, gain, (2 * bw + 3 * bw) // 512)
            a_all = _attention(e_sinks[i], q16, kt16, vt16, s)
            conv8 = jnp.pad(e_conv_w[i], ((0, 8 - CONV_W), (0, 0)))
            qkv = _dn_prepare(proj, conv8, s, 2)
            gate_block = (n_pad - LANES) // LANES
            gates_t = jnp.swapaxes(proj[:, :, n_pad - LANES:n_pad - LANES + 4 * B_HEADS], 1, 2)
            o_f = _dn_scan(qkv, proj, gates_t, e_a_log[i], e_dt_bias[i], None, s, gate_block, rev=False)
            o_mix = _dn_scan(qkv, proj, gates_t, e_a_log[i], e_dt_bias[i], o_f, s, gate_block, rev=True)
            gnorm = jnp.tile(e_dn_norm[i], B_HEADS).reshape(1, bw)
            x_all, h2, logits = _mixer_output(a_all, o_mix, proj, 1, gnorm, e_w_out[i].astype(bf16), x_all, mod_l,
                                              norm_g[layer, 1], wr[layer], br[layer], s, B_HEAD_DIM)
        else:
            w = o_w_in[i]
            n_pad = 6272
            proj = _input_projection(x_all, norm_g[layer, 0], mod_l, _pad_cols(w, n_pad).astype(bf16), s, 896)
            dk_all = o_gate_w2.shape[3]
            o_mix = None
            for dirn in range(2):
                w2pad = jnp.zeros((LANES, dk_all), f32).at[dirn * GATE_RANK:(dirn + 1) * GATE_RANK].set(
                    o_gate_w2[i, dirn])
                o_mix = _gla_scan(proj, w2pad, o_gate_b[i, dirn].reshape(1, dk_all), o_mix, s, rev=dirn == 1)
            dv = o_gla_norm.shape[1]
            gnorm = jnp.tile(o_gla_norm[i], C_HEADS).reshape(1, C_HEADS * dv)
            x_all, h2, logits = _mixer_output(None, o_mix, proj, 2, gnorm, o_w_out[i].astype(bf16), x_all, mod_l,
                                              norm_g[layer, 1], wr[layer], br[layer], s, dv)
        x_all = _moe_layer(h2, logits, x_all, mod_l, w_gu16, b_gu4, w_down16, b_down4, layer, s,
                           s if last else t_all)
    return x_all
```

```python
import functools

import numpy as np
import jax
import jax.numpy as jnp
from jax import lax
from jax.experimental import pallas as pl
from jax.experimental.pallas import tpu as pltpu

f32 = jnp.float32
bf16 = jnp.bfloat16

NORM_EPS = 1e-6
NEG_INF = -1e30
GRID_W = 64
ROPE_BASE = 10000.0
A_HEADS, A_KV_HEADS, A_HEAD_DIM, WINDOW = 16, 4, 64, 128
A_GROUP = A_HEADS // A_KV_HEADS
B_HEADS, B_HEAD_DIM, CONV_W = 8, 128, 5
CHUNK = 64
DN_NEWTON_STEPS = 2
C_HEADS = 4
GATE_RANK = 16
GATE_NORMALIZER = 16.0
TOP_K = 4
SWIGLU_ALPHA, SWIGLU_LIMIT = 1.702, 7.0

LANES = 128
SEQ_BLOCK = 256
VMEM_LIMIT = 56 * 1024 * 1024
MOE_ROWS = 512
MOE_FF_TILE = 1024


def _cparams(*sem):
    return pltpu.CompilerParams(dimension_semantics=sem, vmem_limit_bytes=VMEM_LIMIT)


def _split2(x):
    hi = x.astype(bf16)
    return hi, (x - hi.astype(f32)).astype(bf16)


def _split3(x):
    x1 = x.astype(bf16)
    r1 = x - x1.astype(f32)
    x2 = r1.astype(bf16)
    x3 = (r1 - x2.astype(f32)).astype(bf16)
    return x1, x2, x3


def _mm(a, b):
    return jnp.dot(a, b, preferred_element_type=f32)


def _mm_nt(a, b):
    return lax.dot_general(a, b, (((1,), (1,)), ((), ())), preferred_element_type=f32)


def _mm_lhs01(a01, x):
    x1, x2, x3 = _split3(x)
    return _mm(a01, x1) + _mm(a01, x2) + _mm(a01, x3)


def _mm_rhs01(x, b01):
    x1, x2, x3 = _split3(x)
    return _mm(x1, b01) + _mm(x2, b01) + _mm(x3, b01)


def _mm_x3(a, b):
    a1, a2 = _split2(a)
    b1, b2 = _split2(b)
    return _mm(a1, b1) + _mm(a1, b2) + _mm(a2, b1)


def _sigmoid(x):
    return 1.0 / (1.0 + jnp.exp(-x))


def _softplus(x):
    return jnp.maximum(x, 0.0) + jnp.log(1.0 + jnp.exp(-jnp.abs(x)))


def _mask01(m):
    return jnp.where(m, 1.0, 0.0).astype(bf16)


def _chunk_masks(n, rev):
    row = lax.broadcasted_iota(jnp.int32, (n, n), 0)
    col = lax.broadcasted_iota(jnp.int32, (n, n), 1)
    same = (row // CHUNK) == (col // CHUNK)
    if rev:
        return same, same & (col >= row), same & (col > row), row, col
    return same, same & (col <= row), same & (col < row), row, col


def _mod_kernel(c_ref, w_ref, b_ref, o_ref):
    c = c_ref[...]
    o_ref[0] = _mm_x3(c * _sigmoid(c), w_ref[0]) + b_ref[0]


def _mod_vectors(c8, w_mod, b_mod):
    depth, d, n = w_mod.shape
    tn = 1024
    return pl.pallas_call(
        _mod_kernel,
        grid=(depth, n // tn),
        in_specs=[pl.BlockSpec((8, d), lambda l, j: (0, 0)),
                  pl.BlockSpec((1, d, tn), lambda l, j: (l, 0, j)),
                  pl.BlockSpec((1, 1, tn), lambda l, j: (l, 0, j))],
        out_specs=pl.BlockSpec((1, 8, tn), lambda l, j: (l, 0, j)),
        out_shape=jax.ShapeDtypeStruct((depth, 8, n), f32),
        compiler_params=_cparams("parallel", "parallel"),
        name="mod_vectors",
    )(c8, w_mod, b_mod.reshape(depth, 1, n))


def _proj_kernel(x_ref, g_ref, mb_ref, mc_ref, w_ref, o_ref, h_ref, *, tm, n_lat, sub):
    i = pl.program_id(1)

    @pl.when(pl.program_id(2) == 0)
    def _():
        for r0 in range(0, tm, sub):
            x = x_ref[0, r0:r0 + sub, :]
            y = x * lax.rsqrt(jnp.mean(x * x, axis=-1, keepdims=True) + NORM_EPS) * g_ref[...]
            row = i * tm + r0 + lax.broadcasted_iota(jnp.int32, (sub, 1), 0)
            is_ctx = row >= n_lat
            shift = jnp.where(is_ctx, mc_ref[0, 0:1, :], mb_ref[0, 0:1, :])
            scale = jnp.where(is_ctx, mc_ref[0, 1:2, :], mb_ref[0, 1:2, :])
            h_ref[r0:r0 + sub, :] = (y * (1.0 + scale) + shift).astype(bf16)

    o_ref[0] = _mm(h_ref[...], w_ref[...])


def _input_projection(x_all, gain, mod_l, w, n_lat, tn):
    b, t_all, d = x_all.shape
    n = w.shape[1]
    tm = 1280 if t_all % 1280 == 0 else SEQ_BLOCK
    assert t_all % tm == 0 and n % tn == 0
    return pl.pallas_call(
        functools.partial(_proj_kernel, tm=tm, n_lat=n_lat, sub=128),
        grid=(b, t_all // tm, n // tn),
        in_specs=[pl.BlockSpec((1, tm, d), lambda bi, i, j: (bi, i, 0)),
                  pl.BlockSpec((1, d), lambda bi, i, j: (0, 0)),
                  pl.BlockSpec((1, 6, d), lambda bi, i, j: (bi, 0, 0)),
                  pl.BlockSpec((1, 6, d), lambda bi, i, j: (2, 0, 0)),
                  pl.BlockSpec((d, tn), lambda bi, i, j: (0, j))],
        out_specs=pl.BlockSpec((1, tm, tn), lambda bi, i, j: (bi, i, j)),
        out_shape=jax.ShapeDtypeStruct((b, t_all, n), f32),
        scratch_shapes=[pltpu.VMEM((tm, d), bf16)],
        compiler_params=_cparams("parallel", "parallel", "arbitrary"),
        name="input_projection",
    )(x_all, gain.reshape(1, d), mod_l, mod_l, w)


def _qkprep_kernel(q_ref, kv_ref, cos_ref, sin_ref, gain_ref, ones_ref, e_ref, qo_ref, ko_ref, vo_ref):
    cos = cos_ref[...]
    sin = sin_ref[...]
    lane = lax.broadcasted_iota(jnp.int32, cos.shape, 1)
    upper = (lane & 32) != 0
    ones = ones_ref[...]

    def norm_rope(x, gain):
        ms = _mm_rhs01(x * x, ones) * (1.0 / A_HEAD_DIM)
        y = x * lax.rsqrt(ms + NORM_EPS) * gain
        outs = []
        for c in range(2):
            yc = y[:, c * LANES:(c + 1) * LANES]
            partner = jnp.where(upper, pltpu.roll(yc, 32, 1), pltpu.roll(yc, 96, 1))
            outs.append(yc * cos + partner * sin)
        return jnp.concatenate(outs, axis=1)

    for c in range(4):
        sl = slice(c * 256, (c + 1) * 256)
        qo_ref[0, :, sl] = norm_rope(q_ref[0, :, sl], gain_ref[:, sl]).astype(bf16)
    k = norm_rope(kv_ref[0, :, 0:256], gain_ref[:, 1024:1280]).astype(bf16)
    ko_ref[0] = _mm(k, e_ref[...]).astype(bf16)
    vo_ref[0] = _mm(kv_ref[0, :, 256:512].astype(bf16), e_ref[...]).astype(bf16)


def _qk_prepare(proj, cos_t, sin_t, gain, kv_block):
    b, t_all, _ = proj.shape
    tt = SEQ_BLOCK
    hd = A_HEAD_DIM
    ones = np.kron(np.eye(4, dtype=np.float32), np.ones((hd, hd), np.float32))
    expand = np.kron(np.eye(4, dtype=np.float32), np.tile(np.eye(hd, dtype=np.float32), (1, A_GROUP)))
    out = jax.ShapeDtypeStruct((b, t_all, 1024), bf16)
    return pl.pallas_call(
        _qkprep_kernel,
        grid=(b, t_all // tt),
        in_specs=[pl.BlockSpec((1, tt, 1024), lambda bi, i: (bi, i, 0)),
                  pl.BlockSpec((1, tt, 512), lambda bi, i: (bi, i, kv_block)),
                  pl.BlockSpec((tt, LANES), lambda bi, i: (i, 0)),
                  pl.BlockSpec((tt, LANES), lambda bi, i: (i, 0)),
                  pl.BlockSpec((1, 1280), lambda bi, i: (0, 0)),
                  pl.BlockSpec((256, 256), lambda bi, i: (0, 0)),
                  pl.BlockSpec((256, 1024), lambda bi, i: (0, 0))],
        out_specs=[pl.BlockSpec((1, tt, 1024), lambda bi, i: (bi, i, 0))] * 3,
        out_shape=[out, out, out],
        compiler_params=_cparams("parallel", "parallel"),
        name="qk_prepare",
    )(proj, proj, cos_t, sin_t, gain, jnp.asarray(ones, bf16), jnp.asarray(expand, bf16))


def _attn_kernel(sink_ref, q_ref, kp_ref, kc_ref, kn_ref, kx_ref, vp_ref, vc_ref, vn_ref, vx_ref, o_ref, *,
                 nq_lat, n_ctx):
    n = pl.program_id(1)
    blk = WINDOW
    rows = A_GROUP * blk
    nband = 3 * blk
    ncol = nband + n_ctx
    rq = lax.broadcasted_iota(jnp.int32, (rows, ncol), 0) % blk
    col = lax.broadcasted_iota(jnp.int32, (rows, ncol), 1)
    rgrp = lax.broadcasted_iota(jnp.int32, (rows, 1), 0) // blk
    dist = col - rq
    is_lat = n < nq_lat
    lo = jnp.where(n > 0, 0, blk)
    hi = jnp.where(is_lat, jnp.where(n < nq_lat - 1, nband, 2 * blk), 0)
    allowed = (col >= nband) | ((dist >= 0) & (dist <= 2 * WINDOW) & (col >= lo) & (col < hi))
    lane_g = lax.broadcasted_iota(jnp.int32, (blk, A_GROUP * A_HEAD_DIM), 1) // A_HEAD_DIM
    for h in range(A_KV_HEADS):
        sl = slice(h * 256, (h + 1) * 256)
        qh = q_ref[0, :, sl]
        qs = jnp.concatenate([jnp.where(lane_g == g, qh, jnp.zeros_like(qh)) for g in range(A_GROUP)], axis=0)
        kcat = jnp.concatenate([kp_ref[0, :, sl], kc_ref[0, :, sl], kn_ref[0, :, sl], kx_ref[0, :, sl]], axis=0)
        vcat = jnp.concatenate([vp_ref[0, :, sl], vc_ref[0, :, sl], vn_ref[0, :, sl], vx_ref[0, :, sl]], axis=0)
        s = jnp.where(allowed, _mm_nt(qs, kcat), NEG_INF)
        sink = jnp.zeros((rows, 1), f32)
        for g in range(A_GROUP):
            sink = jnp.where(rgrp == g, sink_ref[h * A_GROUP + g], sink)
        m = jnp.maximum(jnp.max(s, axis=-1, keepdims=True), sink)
        p = jnp.exp(s - m)
        den = jnp.sum(p, axis=-1, keepdims=True) + jnp.exp(sink - m)
        o = _mm(p.astype(bf16), vcat) * (1.0 / den)
        out = jnp.zeros((blk, 256), f32)
        for g in range(A_GROUP):
            out = out + jnp.where(lane_g == g, o[g * blk:(g + 1) * blk, :], 0.0)
        o_ref[0, :, sl] = out.astype(bf16)


def _attention(sinks, q, kt, vt, n_lat):
    b, t_all, w = q.shape
    blk = WINDOW
    nq = t_all // blk
    n_ctx = t_all - n_lat
    ctx_blk = n_lat // n_ctx
    qspec = pl.BlockSpec((1, blk, w), lambda bi, i: (bi, i, 0))
    prev = pl.BlockSpec((1, blk, w), lambda bi, i: (bi, jnp.maximum(i - 1, 0), 0))
    nxt = pl.BlockSpec((1, blk, w), lambda bi, i: (bi, jnp.minimum(i + 1, nq - 1), 0))
    ctx = pl.BlockSpec((1, n_ctx, w), lambda bi, i: (bi, ctx_blk, 0))
    return pl.pallas_call(
        functools.partial(_attn_kernel, nq_lat=n_lat // blk, n_ctx=n_ctx),
        grid=(b, nq),
        in_specs=[pl.BlockSpec(memory_space=pltpu.SMEM), qspec, prev, qspec, nxt, ctx, prev, qspec, nxt, ctx],
        out_specs=qspec,
        out_shape=jax.ShapeDtypeStruct((b, t_all, w), bf16),
        compiler_params=_cparams("parallel", "parallel"),
        name="window_attention",
    )(sinks, q, kt, kt, kt, kt, vt, vt, vt, vt)


def _dnprep_kernel(x_ref, xp_ref, xn_ref, w_ref, o_ref, ext_ref, *, nt_lat, tt):
    i = pl.program_id(1)
    j = pl.program_id(2)
    keep_prev = jnp.where((i == 0) | (i == nt_lat), 0.0, 1.0)
    keep_next = jnp.where((i == nt_lat - 1) | (i == nt_lat), 0.0, 1.0)
    ext_ref[0:8, :] = xp_ref[0] * keep_prev
    ext_ref[8:8 + tt, :] = x_ref[0]
    ext_ref[8 + tt:16 + tt, :] = xn_ref[0] * keep_next
    half = CONV_W // 2
    acc = jnp.zeros((tt, x_ref.shape[2]), f32)
    for tap in range(CONV_W):
        acc = acc + ext_ref[pl.ds(8 - half + tap, tt), :] * w_ref[tap:tap + 1, :]
    y = acc * _sigmoid(acc)
    q_scale = jnp.where(j == 0, B_HEAD_DIM ** -0.5, 1.0)
    for h in range(x_ref.shape[2] // B_HEAD_DIM):
        sl = slice(h * B_HEAD_DIM, (h + 1) * B_HEAD_DIM)
        yh = y[:, sl]
        rn = lax.rsqrt(jnp.sum(yh * yh, axis=-1, keepdims=True) + 1e-6) * q_scale
        o_ref[0, :, sl] = yh * jnp.where(j == 2, 1.0, rn)


def _dn_prepare(proj, conv_w8, n_lat, col_block):
    b, t_all, _ = proj.shape
    tt = SEQ_BLOCK
    bw = B_HEADS * B_HEAD_DIM
    r8 = tt // 8
    return pl.pallas_call(
        functools.partial(_dnprep_kernel, nt_lat=n_lat // tt, tt=tt),
        grid=(b, t_all // tt, 3),
        in_specs=[pl.BlockSpec((1, tt, bw), lambda bi, i, j: (bi, i, col_block + j)),
                  pl.BlockSpec((1, 8, bw), lambda bi, i, j: (bi, jnp.maximum(i * r8 - 1, 0), col_block + j)),
                  pl.BlockSpec((1, 8, bw), lambda bi, i, j: (bi, jnp.minimum((i + 1) * r8, t_all // 8 - 1),
                                                             col_block + j)),
                  pl.BlockSpec((8, bw), lambda bi, i, j: (0, j))],
        out_specs=pl.BlockSpec((1, tt, bw), lambda bi, i, j: (bi, i, j)),
        out_shape=jax.ShapeDtypeStruct((b, t_all, 3 * bw), f32),
        scratch_shapes=[pltpu.VMEM((tt + 16, bw), f32)],
        compiler_params=_cparams("parallel", "parallel", "parallel"),
        name="deltanet_prepare",
    )(proj, proj, proj, conv_w8)


def _dnscan_kernel(*refs, rev, add_prev):
    if add_prev:
        (q_ref, k_ref, v_ref, gc_ref, gt_ref, alr_ref, dtr_ref, alc_ref, dtc_ref, op_ref, o_ref,
         s_ref, u_scr, w_scr, qd_scr, kdt_scr, at_scr, vn_scr, egl_scr) = refs
    else:
        (q_ref, k_ref, v_ref, gc_ref, gt_ref, alr_ref, dtr_ref, alc_ref, dtc_ref, o_ref,
         s_ref, u_scr, w_scr, qd_scr, kdt_scr, at_scr, vn_scr, egl_scr) = refs
        op_ref = None
    tt = q_ref.shape[1]
    hd = B_HEAD_DIM
    d = 1 if rev else 0

    @pl.when(pl.program_id(1) == 0)
    def _():
        s_ref[...] = jnp.zeros(s_ref.shape, f32)

    same, incl, strict, row, col = _chunk_masks(tt, rev)
    tri = _mask01(incl)
    tri_t = _mask01(same & ((row >= col) if rev else (row <= col)))
    ones_bd = _mask01(same)
    eye = jnp.where(row == col, 1.0, 0.0)

    gates = gc_ref[0]
    g_all = -jnp.exp(alr_ref[...]) * _softplus(gates + dtr_ref[...])
    beta_all = _sigmoid(gates)
    gcum = _mm_lhs01(tri, g_all)
    gsum = _mm_lhs01(ones_bd, g_all)
    egl_scr[...] = jnp.exp(gsum)
    g_row = -jnp.exp(alc_ref[...]) * _softplus(gt_ref[0] + dtc_ref[...])
    gcum_r = _mm_rhs01(g_row, tri_t)

    ms, ps, rhss, m0s = [], [], [], []
    for h in range(B_HEADS):
        li = d * B_HEADS + h
        sl = slice(h * hd, (h + 1) * hd)
        qh, kh, vh = q_ref[0, :, sl], k_ref[0, :, sl], v_ref[0, :, sl]
        gc = gcum[:, li:li + 1]
        gs = gsum[:, li:li + 1]
        beta = beta_all[:, 2 * B_HEADS + li:2 * B_HEADS + li + 1]
        dec = jnp.where(incl, jnp.exp(jnp.where(incl, gc - gcum_r[li:li + 1, :], 0.0)), 0.0)
        kb = kh * beta
        kh16 = kh.astype(bf16)
        m = -jnp.where(strict, _mm_nt(kb.astype(bf16), kh16) * dec, 0.0)
        ms.append(m)
        m0s.append(_split2(m))
        ps.append(eye + m)
        egc = jnp.exp(gc)
        rhss.append(_split2(jnp.concatenate([vh * beta, kb * egc], axis=1)))
        at_scr[h] = jnp.where(incl, _mm_nt(qh.astype(bf16), kh16) * dec, 0.0)
        qd_scr[h] = qh * egc
        kdt_scr[h] = (kh * jnp.exp(gs - gc)).T
    for _ in range(5):
        for h in range(B_HEADS):
            m16 = ms[h].astype(bf16)
            ms[h] = _mm(m16, m16)
        for h in range(B_HEADS):
            ps[h] = ps[h] + _mm(ps[h].astype(bf16), ms[h].astype(bf16))
    for _ in range(DN_NEWTON_STEPS):
        xs = [_split2(ps[h]) for h in range(B_HEADS)]
        prods = [_mm(m0s[h][0], xs[h][0]) + _mm(m0s[h][0], xs[h][1]) + _mm(m0s[h][1], xs[h][0])
                 for h in range(B_HEADS)]
        resids = [((eye - ps[h]) + prods[h]).astype(bf16) for h in range(B_HEADS)]
        for h in range(B_HEADS):
            ps[h] = ps[h] + _mm(xs[h][0], resids[h])
    for h in range(B_HEADS):
        p_hi, p_lo = _split2(ps[h])
        r_hi, r_lo = rhss[h]
        sol = _mm(p_hi, r_hi) + _mm(p_hi, r_lo) + _mm(p_lo, r_hi)
        u_scr[h] = sol[:, :hd]
        vn_scr[h] = sol[:, :hd]
        w_scr[h] = sol[:, hd:]

    colk = lax.broadcasted_iota(jnp.int32, (hd, tt), 1) // CHUNK
    nchunk = tt // CHUNK
    for c in (range(nchunk - 1, -1, -1) if rev else range(nchunk)):
        rs = slice(c * CHUNK, (c + 1) * CHUNK)
        egl_row = egl_scr[c * CHUNK:c * CHUNK + 1, :]
        heads = range(B_HEADS)
        s_all = [s_ref[h] for h in heads]
        s16 = [s.astype(bf16) for s in s_all]
        for h in heads:
            vn_scr[h, rs, :] = u_scr[h, rs, :] - _mm(w_scr[h, rs, :].astype(bf16), s16[h])
        vn16 = [vn_scr[h].astype(bf16) for h in heads]
        for h in heads:
            sl = slice(h * hd, (h + 1) * hd)
            o = _mm(qd_scr[h, rs, :].astype(bf16), s16[h]) + _mm(at_scr[h, rs, :].astype(bf16), vn16[h])
            if op_ref is not None:
                o = o + op_ref[0, rs, sl]
            o_ref[0, rs, sl] = o
        for h in heads:
            li = d * B_HEADS + h
            kdt = jnp.where(colk == c, kdt_scr[h], 0.0).astype(bf16)
            s_ref[h] = s_all[h] * egl_row[:, li:li + 1] + _mm(kdt, vn16[h])


def _seq_block_index(i, nt_lat, rev):
    lat = (nt_lat - i) if rev else (i - 1)
    return jnp.where(i == 0, nt_lat, lat)


def _dn_scan(qkv, proj, gates_t, alog, dtb, o_prev, n_lat, gate_block, rev):
    b, t_all, _ = qkv.shape
    tt = SEQ_BLOCK
    nt_lat = n_lat // tt
    bw = B_HEADS * B_HEAD_DIM
    hd = B_HEAD_DIM
    blk = lambda bi, i: _seq_block_index(i, nt_lat, rev)
    pad = jnp.zeros((LANES - 2 * B_HEADS,), f32)
    alr = jnp.concatenate([alog.reshape(-1), pad]).reshape(1, LANES)
    dtr = jnp.concatenate([dtb.reshape(-1), pad]).reshape(1, LANES)
    alc = jnp.concatenate([alog.reshape(-1), pad[:2 * B_HEADS]]).reshape(4 * B_HEADS, 1)
    dtc = jnp.concatenate([dtb.reshape(-1), pad[:2 * B_HEADS]]).reshape(4 * B_HEADS, 1)
    small = lambda shape: pl.BlockSpec(shape, lambda bi, i: (0, 0))
    in_specs = [pl.BlockSpec((1, tt, bw), lambda bi, i: (bi, blk(bi, i), 0)),
                pl.BlockSpec((1, tt, bw), lambda bi, i: (bi, blk(bi, i), 1)),
                pl.BlockSpec((1, tt, bw), lambda bi, i: (bi, blk(bi, i), 2)),
                pl.BlockSpec((1, tt, LANES), lambda bi, i: (bi, blk(bi, i), gate_block)),
                pl.BlockSpec((1, 4 * B_HEADS, tt), lambda bi, i: (bi, 0, blk(bi, i))),
                small((1, LANES)), small((1, LANES)), small((4 * B_HEADS, 1)), small((4 * B_HEADS, 1))]
    args = [qkv, qkv, qkv, proj, gates_t, alr, dtr, alc, dtc]
    if o_prev is not None:
        in_specs.append(pl.BlockSpec((1, tt, bw), lambda bi, i: (bi, blk(bi, i), 0)))
        args.append(o_prev)
    head = lambda *s: pltpu.VMEM((B_HEADS,) + s, f32)
    return pl.pallas_call(
        functools.partial(_dnscan_kernel, rev=rev, add_prev=o_prev is not None),
        grid=(b, nt_lat + 1),
        in_specs=in_specs,
        out_specs=pl.BlockSpec((1, tt, bw), lambda bi, i: (bi, blk(bi, i), 0)),
        out_shape=jax.ShapeDtypeStruct((b, t_all, bw), f32),
        scratch_shapes=[head(hd, hd), head(tt, hd), head(tt, hd), head(tt, hd), head(hd, tt), head(tt, tt),
                        head(tt, hd), pltpu.VMEM((tt, LANES), f32)],
        compiler_params=_cparams("parallel", "arbitrary"),
        name="deltanet_scan_bwd" if rev else "deltanet_scan_fwd",
    )(*args)


def _glascan_kernel(*refs, rev, add_prev):
    if add_prev:
        q_ref, k_ref, v_ref, r_ref, w2_ref, gb_ref, op_ref, o_ref, s_ref = refs
    else:
        q_ref, k_ref, v_ref, r_ref, w2_ref, gb_ref, o_ref, s_ref = refs
        op_ref = None
    tt = q_ref.shape[1]
    dk = q_ref.shape[2] // C_HEADS
    dv = v_ref.shape[2] // C_HEADS
    heads = range(C_HEADS)

    @pl.when(pl.program_id(1) == 0)
    def _():
        s_ref[...] = jnp.zeros(s_ref.shape, f32)

    same, incl, _, _, _ = _chunk_masks(tt, rev)
    tri = _mask01(incl)
    ones_bd = _mask01(same)
    z = _mm_x3(r_ref[0], w2_ref[...]) + gb_ref[...]
    gk = (jnp.minimum(z, 0.0) - jnp.log(1.0 + jnp.exp(-jnp.abs(z)))) * (1.0 / GATE_NORMALIZER)
    bcum = _mm_lhs01(tri, gk)
    bsum = _mm_lhs01(ones_bd, gk)
    nchunk = tt // CHUNK
    mid = (CHUNK - 1 - CHUNK // 2) if rev else CHUNK // 2
    bmid = jnp.concatenate(
        [jnp.broadcast_to(bcum[c * CHUNK + mid:c * CHUNK + mid + 1, :], (CHUNK, bcum.shape[1]))
         for c in range(nchunk)], axis=0)
    q = q_ref[0] * (dk ** -0.5)
    k = k_ref[0]
    v16 = v_ref[0].astype(bf16)
    qe = (q * jnp.exp(bcum - bmid)).astype(bf16)
    ke = (k * jnp.exp(bmid - bcum)).astype(bf16)
    q_dec = (q * jnp.exp(bcum)).astype(bf16)
    kd = k * jnp.exp(bsum - bcum)
    dec = jnp.exp(bsum)
    ks = [slice(h * dk, (h + 1) * dk) for h in heads]
    vs = [slice(h * dv, (h + 1) * dv) for h in heads]
    attn = [jnp.where(incl, _mm_nt(qe[:, ks[h]], ke[:, ks[h]]), 0.0).astype(bf16) for h in heads]
    o_intra = [_mm(attn[h], v16[:, vs[h]]) for h in heads]
    kd_t = [kd[:, ks[h]].T for h in heads]
    dec_t = [dec[:, ks[h]].T for h in heads]
    colk = lax.broadcasted_iota(jnp.int32, (dk, tt), 1) // CHUNK
    for c in (range(nchunk - 1, -1, -1) if rev else range(nchunk)):
        rs = slice(c * CHUNK, (c + 1) * CHUNK)
        s_all = [s_ref[h] for h in heads]
        for h in heads:
            o = o_intra[h][rs, :] + _mm(q_dec[rs, ks[h]], s_all[h].astype(bf16))
            if op_ref is not None:
                o = o + op_ref[0, rs, vs[h]]
            o_ref[0, rs, vs[h]] = o
        for h in heads:
            kdt = jnp.where(colk == c, kd_t[h], 0.0).astype(bf16)
            s_ref[h] = s_all[h] * dec_t[h][:, c * CHUNK:c * CHUNK + 1] + _mm(kdt, v16[:, vs[h]])


def _gla_scan(proj, w2pad, gate_b, o_prev, n_lat, rev):
    b, t_all, _ = proj.shape
    tt = SEQ_BLOCK
    nt_lat = n_lat // tt
    dk_all = w2pad.shape[1]
    dk = dk_all // C_HEADS
    dv_all = 2 * dk_all
    blk = lambda i: _seq_block_index(i, nt_lat, rev)
    in_specs = [pl.BlockSpec((1, tt, dk_all), lambda bi, i: (bi, blk(i), 0)),
                pl.BlockSpec((1, tt, dk_all), lambda bi, i: (bi, blk(i), 1)),
                pl.BlockSpec((1, tt, dv_all), lambda bi, i: (bi, blk(i), 1)),
                pl.BlockSpec((1, tt, LANES), lambda bi, i: (bi, blk(i), (2 * dk_all + 2 * dv_all) // LANES)),
                pl.BlockSpec((LANES, dk_all), lambda bi, i: (0, 0)),
                pl.BlockSpec((1, dk_all), lambda bi, i: (0, 0))]
    args = [proj, proj, proj, proj, w2pad, gate_b]
    if o_prev is not None:
        in_specs.append(pl.BlockSpec((1, tt, dv_all), lambda bi, i: (bi, blk(i), 0)))
        args.append(o_prev)
    return pl.pallas_call(
        functools.partial(_glascan_kernel, rev=rev, add_prev=o_prev is not None),
        grid=(b, nt_lat + 1),
        in_specs=in_specs,
        out_specs=pl.BlockSpec((1, tt, dv_all), lambda bi, i: (bi, blk(i), 0)),
        out_shape=jax.ShapeDtypeStruct((b, t_all, dv_all), f32),
        scratch_shapes=[pltpu.VMEM((C_HEADS, dk, 2 * dk), f32)],
        compiler_params=_cparams("parallel", "arbitrary"),
        name="gla_scan_bwd" if rev else "gla_scan_fwd",
    )(*args)


def _out_kernel(*refs, group, has_a):
    if has_a:
        a_ref, o_ref, z_ref, gn_ref, w_ref, x_ref, mod_ref, g2_ref, wr_ref, br_ref, xo_ref, ho_ref, lo_ref = refs
    else:
        o_ref, z_ref, gn_ref, w_ref, x_ref, mod_ref, g2_ref, wr_ref, br_ref, xo_ref, ho_ref, lo_ref = refs
    width = o_ref.shape[2]
    z = z_ref[0]
    gate = z * _sigmoid(z)
    parts = []
    for g0 in range(0, width, group):
        og = o_ref[0, :, g0:g0 + group]
        parts.append(og * lax.rsqrt(jnp.mean(og * og, axis=-1, keepdims=True) + NORM_EPS))
    on = (jnp.concatenate(parts, axis=1) * gn_ref[...] * gate).astype(bf16)
    if has_a:
        na = a_ref.shape[2]
        y = _mm(a_ref[0], w_ref[0:na, :]) + _mm(on, w_ref[na:na + width, :])
    else:
        y = _mm(on, w_ref[...])
    xn = x_ref[0] + mod_ref[0, 2:3, :] * y
    xo_ref[0] = xn
    hn = xn * lax.rsqrt(jnp.mean(xn * xn, axis=-1, keepdims=True) + NORM_EPS) * g2_ref[...]
    h2 = hn * (1.0 + mod_ref[0, 4:5, :]) + mod_ref[0, 3:4, :]
    ho_ref[0] = h2
    lo_ref[0] = _mm_x3(h2, wr_ref[...]) + br_ref[...]


def _mixer_output(a, o, proj, z_block, gnorm, w_out, x_all, mod_l, g2, wr, br, n_lat, group):
    b, t_all, d = x_all.shape
    tm = SEQ_BLOCK
    nt_lat = n_lat // tm
    width = o.shape[2]
    row = lambda w: pl.BlockSpec((1, tm, w), lambda bi, i: (bi, i, 0))
    const = lambda shape: pl.BlockSpec(shape, lambda bi, i: (0,) * len(shape))
    in_specs, args = [], []
    if a is not None:
        in_specs.append(row(a.shape[2]))
        args.append(a)
    in_specs += [row(width),
                 pl.BlockSpec((1, tm, width), lambda bi, i: (bi, i, z_block)),
                 const((1, width)), const(w_out.shape), row(d),
                 pl.BlockSpec((1, 6, d), lambda bi, i: (jnp.where(i >= nt_lat, 2, bi), 0, 0)),
                 const((1, d)), const(wr.shape), const((1, LANES))]
    args += [o, proj, gnorm, w_out, x_all, mod_l, g2.reshape(1, d), wr, br]
    return pl.pallas_call(
        functools.partial(_out_kernel, group=group, has_a=a is not None),
        grid=(b, t_all // tm),
        in_specs=in_specs,
        out_specs=[row(d), row(d), row(LANES)],
        out_shape=[jax.ShapeDtypeStruct((b, t_all, d), f32), jax.ShapeDtypeStruct((b, t_all, d), f32),
                   jax.ShapeDtypeStruct((b, t_all, LANES), f32)],
        compiler_params=_cparams("parallel", "parallel"),
        name="mixer_output",
    )(*args)


def _moe_kernel(be_ref, code_ref, nused_ref, h_hbm, wg_ref, wl_ref, bg_ref, bl_ref, wd_ref, bd_ref, sw_ref, y_hbm,
                xbuf, x16, acc, gsem, ssem, *, tm, nf, n_tok):
    m = pl.program_id(0)
    f = pl.program_id(1)
    nu = nused_ref[0]
    active = m < nu
    cur = m % 2
    oth = 1 - cur
    rows = tm // nf

    def gather_start(blk, buf, r):
        tok = code_ref[blk * tm + r] & 0xFFFF
        pltpu.make_async_copy(h_hbm.at[pl.ds(tok, 1)], xbuf.at[buf, pl.ds(r, 1)], gsem.at[buf]).start()

    def gather_wait(buf):
        pltpu.make_async_copy(h_hbm.at[pl.ds(0, tm)], xbuf.at[buf], gsem.at[buf]).wait()

    def scatter_start(blk, buf, r, real):
        code = code_ref[blk * tm + r]
        dst = ((code >> 16) & 7) * n_tok + (code & 0xFFFF)
        dst = jnp.where(real, dst, TOP_K * n_tok + r)
        pltpu.make_async_copy(acc.at[buf, pl.ds(r, 1)], y_hbm.at[pl.ds(dst, 1)], ssem).start()

    def scatter_wait():
        pltpu.make_async_copy(acc.at[0], y_hbm.at[pl.ds(0, tm)], ssem).wait()

    @pl.when(active & (f == 0))
    def _():
        @pl.when(m == 0)
        def _():
            def issue(r, carry):
                gather_start(0, 0, r)
                return carry
            lax.fori_loop(0, tm, issue, 0)
            acc[1] = jnp.zeros(acc.shape[1:], f32)

        gather_wait(cur)

        @pl.when(m > 0)
        def _():
            scatter_wait()

        x16[...] = xbuf[cur].astype(bf16)

    nxt = jnp.minimum(m + 1, pl.num_programs(0) - 1)
    prv = jnp.maximum(m - 1, 0)
    for fs in range(nf):
        @pl.when(active & (f == fs))
        def _(fs=fs):
            for r in range(fs * rows, (fs + 1) * rows):
                gather_start(nxt, oth, r)
                scatter_start(prv, oth, r, m > 0)
            x = x16[...]
            glu = jnp.minimum(_mm(x, wg_ref[0]) + bg_ref[0], SWIGLU_LIMIT)
            lin = jnp.clip(_mm(x, wl_ref[0]) + bl_ref[0], -SWIGLU_LIMIT, SWIGLU_LIMIT)
            act = glu * _sigmoid(SWIGLU_ALPHA * glu) * (lin + 1.0)
            y = _mm(act.astype(bf16), wd_ref[0])
            if fs > 0:
                y = acc[cur] + y
            if fs == nf - 1:
                y = (y + bd_ref[0]) * sw_ref[...]
            acc[cur] = y

    @pl.when((m == nu - 1) & (f == nf - 1))
    def _():
        scatter_wait()

        def issue(r, carry):
            scatter_start(m, cur, r, True)
            return carry
        lax.fori_loop(0, tm, issue, 0)
        scatter_wait()
        gather_wait(oth)


def _moe_experts(h2, block_expert, slot_code, n_used, slot_w, w_gu, b_gu, w_down, b_down, layer):
    n_tok, d = h2.shape
    _, n_exp, _, ff2 = w_gu.shape
    ff = ff2 // 2
    tm, tf = MOE_ROWS, min(MOE_FF_TILE, ff)
    nf = ff // tf
    n_blocks = block_expert.shape[0]
    assert n_tok < (1 << 16) and tm % nf == 0

    def live_expert(m, be, nu):
        return be[jnp.minimum(m, nu[0] - 1)]

    def live_tile(m, f, nu):
        return jnp.where(m < nu[0], f, nf - 1)

    def wmap(off):
        return lambda m, f, be, code, nu: (layer, live_expert(m, be, nu), 0, off + live_tile(m, f, nu))

    grid_spec = pltpu.PrefetchScalarGridSpec(
        num_scalar_prefetch=3,
        grid=(n_blocks, nf),
        in_specs=[pl.BlockSpec(memory_space=pl.ANY),
                  pl.BlockSpec((None, 1, d, tf), wmap(0)),
                  pl.BlockSpec((None, 1, d, tf), wmap(nf)),
                  pl.BlockSpec((None, 1, 1, tf), wmap(0)),
                  pl.BlockSpec((None, 1, 1, tf), wmap(nf)),
                  pl.BlockSpec((None, 1, tf, d),
                               lambda m, f, be, code, nu: (layer, live_expert(m, be, nu), live_tile(m, f, nu), 0)),
                  pl.BlockSpec((None, 1, 1, d), lambda m, f, be, code, nu: (layer, live_expert(m, be, nu), 0, 0)),
                  pl.BlockSpec((tm, 1), lambda m, f, be, code, nu: (jnp.minimum(m, nu[0] - 1), 0))],
        out_specs=pl.BlockSpec(memory_space=pl.ANY),
        scratch_shapes=[pltpu.VMEM((2, tm, d), f32), pltpu.VMEM((tm, d), bf16), pltpu.VMEM((2, tm, d), f32),
                        pltpu.SemaphoreType.DMA((2,)), pltpu.SemaphoreType.DMA(())])
    return pl.pallas_call(
        functools.partial(_moe_kernel, tm=tm, nf=nf, n_tok=n_tok),
        grid_spec=grid_spec,
        out_shape=jax.ShapeDtypeStruct((TOP_K * n_tok + tm, d), f32),
        compiler_params=_cparams("arbitrary", "arbitrary"),
        name="moe_experts",
    )(block_expert, slot_code, n_used, h2, w_gu, w_gu, b_gu, b_gu, w_down, b_down, slot_w)


def _combine_kernel(y0_ref, y1_ref, y2_ref, y3_ref, x_ref, mod_ref, o_ref):
    f = (y0_ref[...] + y1_ref[...]) + (y2_ref[...] + y3_ref[...])
    o_ref[0] = x_ref[0] + mod_ref[0, 5:6, :] * f


def _moe_combine(y4, x_all, mod_l, n_lat, n_rows_out):
    b, t_all, d = x_all.shape
    tb = SEQ_BLOCK
    nt_lat = n_lat // tb
    per_k = b * t_all // tb
    per_b = t_all // tb
    yspec = lambda k: pl.BlockSpec((tb, d), lambda bi, i: (k * per_k + bi * per_b + i, 0))
    return pl.pallas_call(
        _combine_kernel,
        grid=(b, n_rows_out // tb),
        in_specs=[yspec(0), yspec(1), yspec(2), yspec(3),
                  pl.BlockSpec((1, tb, d), lambda bi, i: (bi, i, 0)),
                  pl.BlockSpec((1, 6, d), lambda bi, i: (jnp.where(i >= nt_lat, 2, bi), 0, 0))],
        out_specs=pl.BlockSpec((1, tb, d), lambda bi, i: (bi, i, 0)),
        out_shape=jax.ShapeDtypeStruct((b, n_rows_out, d), f32),
        compiler_params=_cparams("parallel", "parallel"),
        name="moe_combine",
    )(y4, y4, y4, y4, x_all, mod_l)


def _route(logits, n_exp):
    n = logits.shape[0]
    tm = MOE_ROWS
    i32 = jnp.int32
    top_val, top_idx = lax.top_k(logits, TOP_K)
    gate = jax.nn.softmax(top_val, axis=-1)
    e_flat = top_idx.reshape(-1).astype(i32)
    order = jnp.argsort(e_flat).astype(i32)
    counts = jnp.sum((e_flat[None, :] == jnp.arange(n_exp, dtype=i32)[:, None]).astype(i32), axis=1)
    padded = (counts + tm - 1) // tm * tm
    start = jnp.cumsum(counts) - counts
    pad_end = jnp.cumsum(padded)
    pad_start = pad_end - padded
    n_blocks = -(-(n * TOP_K) // tm) + n_exp
    blk_start = jnp.arange(n_blocks, dtype=i32) * tm
    block_expert = jnp.minimum(jnp.sum((pad_end[None, :] <= blk_start[:, None]).astype(i32), axis=1), n_exp - 1)
    n_used = (pad_end[-1] // tm).astype(i32).reshape(1)
    row = jnp.arange(tm, dtype=i32)[None, :]
    off = blk_start[:, None] + row - pad_start[block_expert][:, None]
    valid = (off < counts[block_expert][:, None]) & (blk_start[:, None] < pad_end[-1])
    j = jnp.clip(start[block_expert][:, None] + off, 0, n * TOP_K - 1)
    item = order[j]
    code = jnp.where(valid, (item // TOP_K) | ((item % TOP_K) << 16), row | (TOP_K << 16)).astype(i32)
    slot_w = jnp.where(valid, gate.reshape(-1)[item], 0.0).astype(f32)
    return block_expert.astype(i32), code.reshape(-1), n_used, slot_w.reshape(-1, 1)


def _rope_perm():
    q = A_HEAD_DIM // 4
    one = np.concatenate([np.arange(0, q), np.arange(2 * q, 3 * q), np.arange(q, 2 * q), np.arange(3 * q, 4 * q)])
    return one


def _rope_tables(n_lat, n_ctx):
    rows = n_lat // GRID_W
    row = jnp.repeat(jnp.arange(rows, dtype=f32), GRID_W)
    col = jnp.tile(jnp.arange(GRID_W, dtype=f32), rows)
    n_freq = A_HEAD_DIM // 4
    inv = ROPE_BASE ** (-jnp.arange(n_freq, dtype=f32) / n_freq)
    ang = jnp.concatenate([row[:, None] * inv, col[:, None] * inv], axis=1)
    cos, sin = jnp.cos(ang), jnp.sin(ang)
    cos_t = jnp.tile(cos, (1, 4))
    sin_t = jnp.tile(jnp.concatenate([-sin, sin], axis=1), (1, 2))
    cos_t = jnp.concatenate([cos_t, jnp.ones((n_ctx, LANES), f32)], axis=0)
    sin_t = jnp.concatenate([sin_t, jnp.zeros((n_ctx, LANES), f32)], axis=0)
    return cos_t, sin_t


def _pad_cols(w, n):
    return jnp.pad(w, ((0, 0), (0, n - w.shape[1])))


def _moe_layer(h2, logits, x_all, mod_l, w_gu16, b_gu4, w_down16, b_down4, layer, n_lat, n_rows_out):
    b, t_all, d = x_all.shape
    n_exp = w_gu16.shape[1]
    block_expert, slot_code, n_used, slot_w = _route(logits.reshape(b * t_all, LANES)[:, :n_exp], n_exp)
    y4 = _moe_experts(h2.reshape(b * t_all, d), block_expert, slot_code, n_used, slot_w,
                      w_gu16, b_gu4, w_down16, b_down4, layer)
    return _moe_combine(y4, x_all, mod_l, n_lat, n_rows_out)


def kernel(x, c, ctx, c_ctx, w_mod, b_mod, norm_g, e_w_in, e_w_out, e_q_gain, e_k_gain, e_sinks, e_conv_w, e_a_log, e_dt_bias, e_dn_norm, o_w_in, o_gate_w2, o_gate_b, o_gla_norm, o_w_out, w_router, b_router, w_gu, b_gu, w_down, b_down):
    b, s, d = x.shape
    n_ctx = ctx.shape[1]
    depth = w_mod.shape[0]
    n_exp = w_router.shape[2]
    assert n_ctx == SEQ_BLOCK and s % SEQ_BLOCK == 0 and b == 2
    x_all = jnp.concatenate([x, ctx], axis=1)
    t_all = s + n_ctx

    c8 = jnp.concatenate([c, c_ctx[None, :], jnp.zeros((8 - b - 1, d), f32)], axis=0)
    mod = _mod_vectors(c8, w_mod, b_mod).reshape(depth, 8, 6, d)
    wr = jnp.pad(w_router, ((0, 0), (0, 0), (0, LANES - n_exp)))
    br = jnp.pad(b_router, ((0, 0), (0, LANES - n_exp))).reshape(depth, 1, LANES)
    cos_t, sin_t = _rope_tables(s, n_ctx)
    perm = _rope_perm()
    w_gu16, w_down16 = w_gu.astype(bf16), w_down.astype(bf16)
    b_gu4 = b_gu.reshape(depth, n_exp, 1, b_gu.shape[2])
    b_down4 = b_down.reshape(depth, n_exp, 1, d)
    aq = A_HEADS * A_HEAD_DIM
    akv = A_KV_HEADS * A_HEAD_DIM
    bw = B_HEADS * B_HEAD_DIM

    for layer in range(depth):
        i = layer // 2
        mod_l = mod[layer]
        last = layer == depth - 1
        if layer % 2 == 0:
            w = e_w_in[i]
            qcols = (np.arange(A_HEADS)[:, None] * A_HEAD_DIM + perm[None, :]).reshape(-1)
            kcols = aq + (np.arange(A_KV_HEADS)[:, None] * A_HEAD_DIM + perm[None, :]).reshape(-1)
            o_dn = aq + 2 * akv
            o_z = o_dn + 3 * bw
            o_g = o_z + bw
            w_cat = jnp.concatenate([w[:, qcols], w[:, o_z:o_g], w[:, o_dn:o_z], w[:, kcols],
                                     w[:, aq + akv:aq + 2 * akv], w[:, o_g:]], axis=1)
            n_pad = 5760
            w16 = _pad_cols(w_cat, n_pad).astype(bf16)
            proj = _input_projection(x_all, norm_g[layer, 0], mod_l, w16, s, 1152)
            gain = jnp.concatenate([jnp.tile(e_q_gain[i][perm], A_HEADS) * (A_HEAD_DIM ** -0.5),
                                    jnp.tile(e_k_gain[i][perm], A_KV_HEADS)]).reshape(1, aq + akv)
            q16, kt16, vt16 = _qk_prepare(proj, cos_t, sin_t, gain, (2 * bw + 3 * bw) // 512)
            a_all = _attention(e_sinks[i], q16, kt16, vt16, s)
            conv8 = jnp.pad(e_conv_w[i], ((0, 8 - CONV_W), (0, 0)))
            qkv = _dn_prepare(proj, conv8, s, 2)
            gate_block = (n_pad - LANES) // LANES
            gates_t = jnp.swapaxes(proj[:, :, n_pad - LANES:n_pad - LANES + 4 * B_HEADS], 1, 2)
            o_f = _dn_scan(qkv, proj, gates_t, e_a_log[i], e_dt_bias[i], None, s, gate_block, rev=False)
            o_mix = _dn_scan(qkv, proj, gates_t, e_a_log[i], e_dt_bias[i], o_f, s, gate_block, rev=True)
            gnorm = jnp.tile(e_dn_norm[i], B_HEADS).reshape(1, bw)
            x_all, h2, logits = _mixer_output(a_all, o_mix, proj, 1, gnorm, e_w_out[i].astype(bf16), x_all, mod_l,
                                              norm_g[layer, 1], wr[layer], br[layer], s, B_HEAD_DIM)
        else:
            w = o_w_in[i]
            n_pad = 6272
            proj = _input_projection(x_all, norm_g[layer, 0], mod_l, _pad_cols(w, n_pad).astype(bf16), s, 896)
            dk_all = o_gate_w2.shape[3]
            o_mix = None
            for dirn in range(2):
                w2pad = jnp.zeros((LANES, dk_all), f32).at[dirn * GATE_RANK:(dirn + 1) * GATE_RANK].set(
                    o_gate_w2[i, dirn])
                o_mix = _gla_scan(proj, w2pad, o_gate_b[i, dirn].reshape(1, dk_all), o_mix, s, rev=dirn == 1)
            dv = o_gla_norm.shape[1]
            gnorm = jnp.tile(o_gla_norm[i], C_HEADS).reshape(1, C_HEADS * dv)
            x_all, h2, logits = _mixer_output(None, o_mix, proj, 2, gnorm, o_w_out[i].astype(bf16), x_all, mod_l,
                                              norm_g[layer, 1], wr[layer], br[layer], s, dv)
        x_all = _moe_layer(h2, logits, x_all, mod_l, w_gu16, b_gu4, w_down16, b_down4, layer, s,
                           s if last else t_all)
    return x_all
```

```python
import functools

import numpy as np
import jax
import jax.numpy as jnp
from jax import lax
from jax.experimental import pallas as pl
from jax.experimental.pallas import tpu as pltpu

f32 = jnp.float32
bf16 = jnp.bfloat16

NORM_EPS = 1e-6
NEG_INF = -1e30
GRID_W = 64
ROPE_BASE = 10000.0
A_HEADS, A_KV_HEADS, A_HEAD_DIM, WINDOW = 16, 4, 64, 128
A_GROUP = A_HEADS // A_KV_HEADS
B_HEADS, B_HEAD_DIM, CONV_W = 8, 128, 5
CHUNK = 64
DN_NEWTON_STEPS = 2
C_HEADS = 4
GATE_RANK = 16
GATE_NORMALIZER = 16.0
TOP_K = 4
SWIGLU_ALPHA, SWIGLU_LIMIT = 1.702, 7.0

LANES = 128
SEQ_BLOCK = 256
VMEM_LIMIT = 56 * 1024 * 1024
MOE_ROWS = 512
MOE_FF_TILE = 1024


def _cparams(*sem):
    return pltpu.CompilerParams(dimension_semantics=sem, vmem_limit_bytes=VMEM_LIMIT)


def _split2(x):
    hi = x.astype(bf16)
    return hi, (x - hi.astype(f32)).astype(bf16)


def _split3(x):
    x1 = x.astype(bf16)
    r1 = x - x1.astype(f32)
    x2 = r1.astype(bf16)
    x3 = (r1 - x2.astype(f32)).astype(bf16)
    return x1, x2, x3


def _mm(a, b):
    return jnp.dot(a, b, preferred_element_type=f32)


def _mm_nt(a, b):
    return lax.dot_general(a, b, (((1,), (1,)), ((), ())), preferred_element_type=f32)


def _mm_lhs01(a01, x):
    x1, x2, x3 = _split3(x)
    return _mm(a01, x1) + _mm(a01, x2) + _mm(a01, x3)


def _mm_rhs01(x, b01):
    x1, x2, x3 = _split3(x)
    return _mm(x1, b01) + _mm(x2, b01) + _mm(x3, b01)


def _mm_x3(a, b):
    a1, a2 = _split2(a)
    b1, b2 = _split2(b)
    return _mm(a1, b1) + _mm(a1, b2) + _mm(a2, b1)


def _sigmoid(x):
    return 1.0 / (1.0 + jnp.exp(-x))


def _softplus(x):
    return jnp.maximum(x, 0.0) + jnp.log(1.0 + jnp.exp(-jnp.abs(x)))


def _mask01(m):
    return jnp.where(m, 1.0, 0.0).astype(bf16)


def _chunk_masks(n, rev):
    row = lax.broadcasted_iota(jnp.int32, (n, n), 0)
    col = lax.broadcasted_iota(jnp.int32, (n, n), 1)
    same = (row // CHUNK) == (col // CHUNK)
    if rev:
        return same, same & (col >= row), same & (col > row), row, col
    return same, same & (col <= row), same & (col < row), row, col


def _mod_kernel(c_ref, w_ref, b_ref, o_ref):
    c = c_ref[...]
    o_ref[0] = _mm_x3(c * _sigmoid(c), w_ref[0]) + b_ref[0]


def _mod_vectors(c8, w_mod, b_mod):
    depth, d, n = w_mod.shape
    tn = 1024
    return pl.pallas_call(
        _mod_kernel,
        grid=(depth, n // tn),
        in_specs=[pl.BlockSpec((8, d), lambda l, j: (0, 0)),
                  pl.BlockSpec((1, d, tn), lambda l, j: (l, 0, j)),
                  pl.BlockSpec((1, 1, tn), lambda l, j: (l, 0, j))],
        out_specs=pl.BlockSpec((1, 8, tn), lambda l, j: (l, 0, j)),
        out_shape=jax.ShapeDtypeStruct((depth, 8, n), f32),
        compiler_params=_cparams("parallel", "parallel"),
        name="mod_vectors",
    )(c8, w_mod, b_mod.reshape(depth, 1, n))


def _proj_kernel(x_ref, g_ref, mb_ref, mc_ref, w_ref, o_ref, h_ref, *, tm, n_lat, sub):
    i = pl.program_id(1)

    @pl.when(pl.program_id(2) == 0)
    def _():
        for r0 in range(0, tm, sub):
            x = x_ref[0, r0:r0 + sub, :]
            y = x * lax.rsqrt(jnp.mean(x * x, axis=-1, keepdims=True) + NORM_EPS) * g_ref[...]
            row = i * tm + r0 + lax.broadcasted_iota(jnp.int32, (sub, 1), 0)
            is_ctx = row >= n_lat
            shift = jnp.where(is_ctx, mc_ref[0, 0:1, :], mb_ref[0, 0:1, :])
            scale = jnp.where(is_ctx, mc_ref[0, 1:2, :], mb_ref[0, 1:2, :])
            h_ref[r0:r0 + sub, :] = (y * (1.0 + scale) + shift).astype(bf16)

    o_ref[0] = _mm(h_ref[...], w_ref[...])


def _input_projection(x_all, gain, mod_l, w, n_lat, tn):
    b, t_all, d = x_all.shape
    n = w.shape[1]
    tm = 1280 if t_all % 1280 == 0 else SEQ_BLOCK
    assert t_all % tm == 0 and n % tn == 0
    return pl.pallas_call(
        functools.partial(_proj_kernel, tm=tm, n_lat=n_lat, sub=128),
        grid=(b, t_all // tm, n // tn),
        in_specs=[pl.BlockSpec((1, tm, d), lambda bi, i, j: (bi, i, 0)),
                  pl.BlockSpec((1, d), lambda bi, i, j: (0, 0)),
                  pl.BlockSpec((1, 6, d), lambda bi, i, j: (bi, 0, 0)),
                  pl.BlockSpec((1, 6, d), lambda bi, i, j: (2, 0, 0)),
                  pl.BlockSpec((d, tn), lambda bi, i, j: (0, j))],
        out_specs=pl.BlockSpec((1, tm, tn), lambda bi, i, j: (bi, i, j)),
        out_shape=jax.ShapeDtypeStruct((b, t_all, n), f32),
        scratch_shapes=[pltpu.VMEM((tm, d), bf16)],
        compiler_params=_cparams("parallel", "parallel", "arbitrary"),
        name="input_projection",
    )(x_all, gain.reshape(1, d), mod_l, mod_l, w)


def _qkprep_kernel(q_ref, kv_ref, cos_ref, sin_ref, gain_ref, ones_ref, e_ref, qo_ref, ko_ref, vo_ref):
    cos = cos_ref[...]
    sin = sin_ref[...]
    lane = lax.broadcasted_iota(jnp.int32, cos.shape, 1)
    upper = (lane & 32) != 0
    ones = ones_ref[...]

    def norm_rope(x, gain):
        ms = _mm_rhs01(x * x, ones) * (1.0 / A_HEAD_DIM)
        y = x * lax.rsqrt(ms + NORM_EPS) * gain
        outs = []
        for c in range(2):
            yc = y[:, c * LANES:(c + 1) * LANES]
            partner = jnp.where(upper, pltpu.roll(yc, 32, 1), pltpu.roll(yc, 96, 1))
            outs.append(yc * cos + partner * sin)
        return jnp.concatenate(outs, axis=1)

    for c in range(4):
        sl = slice(c * 256, (c + 1) * 256)
        qo_ref[0, :, sl] = norm_rope(q_ref[0, :, sl], gain_ref[:, sl]).astype(bf16)
    k = norm_rope(kv_ref[0, :, 0:256], gain_ref[:, 1024:1280]).astype(bf16)
    ko_ref[0] = _mm(k, e_ref[...]).astype(bf16)
    vo_ref[0] = _mm(kv_ref[0, :, 256:512].astype(bf16), e_ref[...]).astype(bf16)


def _qk_prepare(proj, cos_t, sin_t, gain, kv_block):
    b, t_all, _ = proj.shape
    tt = SEQ_BLOCK
    hd = A_HEAD_DIM
    ones = np.kron(np.eye(4, dtype=np.float32), np.ones((hd, hd), np.float32))
    expand = np.kron(np.eye(4, dtype=np.float32), np.tile(np.eye(hd, dtype=np.float32), (1, A_GROUP)))
    out = jax.ShapeDtypeStruct((b, t_all, 1024), bf16)
    return pl.pallas_call(
        _qkprep_kernel,
        grid=(b, t_all // tt),
        in_specs=[pl.BlockSpec((1, tt, 1024), lambda bi, i: (bi, i, 0)),
                  pl.BlockSpec((1, tt, 512), lambda bi, i: (bi, i, kv_block)),
                  pl.BlockSpec((tt, LANES), lambda bi, i: (i, 0)),
                  pl.BlockSpec((tt, LANES), lambda bi, i: (i, 0)),
                  pl.BlockSpec((1, 1280), lambda bi, i: (0, 0)),
                  pl.BlockSpec((256, 256), lambda bi, i: (0, 0)),
                  pl.BlockSpec((256, 1024), lambda bi, i: (0, 0))],
        out_specs=[pl.BlockSpec((1, tt, 1024), lambda bi, i: (bi, i, 0))] * 3,
        out_shape=[out, out, out],
        compiler_params=_cparams("parallel", "parallel"),
        name="qk_prepare",
    )(proj, proj, cos_t, sin_t, gain, jnp.asarray(ones, bf16), jnp.asarray(expand, bf16))


def _attn_kernel(sink_ref, q_ref, kp_ref, kc_ref, kn_ref, kx_ref, vp_ref, vc_ref, vn_ref, vx_ref, o_ref, *,
                 nq_lat, n_ctx):
    n = pl.program_id(1)
    blk = WINDOW
    rows = A_GROUP * blk
    nband = 3 * blk
    ncol = nband + n_ctx
    rq = lax.broadcasted_iota(jnp.int32, (rows, ncol), 0) % blk
    col = lax.broadcasted_iota(jnp.int32, (rows, ncol), 1)
    rgrp = lax.broadcasted_iota(jnp.int32, (rows, 1), 0) // blk
    dist = col - rq
    is_lat = n < nq_lat
    lo = jnp.where(n > 0, 0, blk)
    hi = jnp.where(is_lat, jnp.where(n < nq_lat - 1, nband, 2 * blk), 0)
    allowed = (col >= nband) | ((dist >= 0) & (dist <= 2 * WINDOW) & (col >= lo) & (col < hi))
    lane_g = lax.broadcasted_iota(jnp.int32, (blk, A_GROUP * A_HEAD_DIM), 1) // A_HEAD_DIM
    for h in range(A_KV_HEADS):
        sl = slice(h * 256, (h + 1) * 256)
        qh = q_ref[0, :, sl]
        qs = jnp.concatenate([jnp.where(lane_g == g, qh, jnp.zeros_like(qh)) for g in range(A_GROUP)], axis=0)
        kcat = jnp.concatenate([kp_ref[0, :, sl], kc_ref[0, :, sl], kn_ref[0, :, sl], kx_ref[0, :, sl]], axis=0)
        vcat = jnp.concatenate([vp_ref[0, :, sl], vc_ref[0, :, sl], vn_ref[0, :, sl], vx_ref[0, :, sl]], axis=0)
        s = jnp.where(allowed, _mm_nt(qs, kcat), NEG_INF)
        sink = jnp.zeros((rows, 1), f32)
        for g in range(A_GROUP):
            sink = jnp.where(rgrp == g, sink_ref[h * A_GROUP + g], sink)
        m = jnp.maximum(jnp.max(s, axis=-1, keepdims=True), sink)
        p = jnp.exp(s - m)
        den = jnp.sum(p, axis=-1, keepdims=True) + jnp.exp(sink - m)
        o = _mm(p.astype(bf16), vcat) * (1.0 / den)
        out = jnp.zeros((blk, 256), f32)
        for g in range(A_GROUP):
            out = out + jnp.where(lane_g == g, o[g * blk:(g + 1) * blk, :], 0.0)
        o_ref[0, :, sl] = out.astype(bf16)


def _attention(sinks, q, kt, vt, n_lat):
    b, t_all, w = q.shape
    blk = WINDOW
    nq = t_all // blk
    n_ctx = t_all - n_lat
    ctx_blk = n_lat // n_ctx
    qspec = pl.BlockSpec((1, blk, w), lambda bi, i: (bi, i, 0))
    prev = pl.BlockSpec((1, blk, w), lambda bi, i: (bi, jnp.maximum(i - 1, 0), 0))
    nxt = pl.BlockSpec((1, blk, w), lambda bi, i: (bi, jnp.minimum(i + 1, nq - 1), 0))
    ctx = pl.BlockSpec((1, n_ctx, w), lambda bi, i: (bi, ctx_blk, 0))
    return pl.pallas_call(
        functools.partial(_attn_kernel, nq_lat=n_lat // blk, n_ctx=n_ctx),
        grid=(b, nq),
        in_specs=[pl.BlockSpec(memory_space=pltpu.SMEM), qspec, prev, qspec, nxt, ctx, prev, qspec, nxt, ctx],
        out_specs=qspec,
        out_shape=jax.ShapeDtypeStruct((b, t_all, w), bf16),
        compiler_params=_cparams("parallel", "parallel"),
        name="window_attention",
    )(sinks, q, kt, kt, kt, kt, vt, vt, vt, vt)


def _dnprep_kernel(x_ref, xp_ref, xn_ref, w_ref, o_ref, ext_ref, *, nt_lat, tt):
    i = pl.program_id(1)
    j = pl.program_id(2)
    keep_prev = jnp.where((i == 0) | (i == nt_lat), 0.0, 1.0)
    keep_next = jnp.where((i == nt_lat - 1) | (i == nt_lat), 0.0, 1.0)
    ext_ref[0:8, :] = xp_ref[0] * keep_prev
    ext_ref[8:8 + tt, :] = x_ref[0]
    ext_ref[8 + tt:16 + tt, :] = xn_ref[0] * keep_next
    half = CONV_W // 2
    acc = jnp.zeros((tt, x_ref.shape[2]), f32)
    for tap in range(CONV_W):
        acc = acc + ext_ref[pl.ds(8 - half + tap, tt), :] * w_ref[tap:tap + 1, :]
    y = acc * _sigmoid(acc)
    q_scale = jnp.where(j == 0, B_HEAD_DIM ** -0.5, 1.0)
    for h in range(x_ref.shape[2] // B_HEAD_DIM):
        sl = slice(h * B_HEAD_DIM, (h + 1) * B_HEAD_DIM)
        yh = y[:, sl]
        rn = lax.rsqrt(jnp.sum(yh * yh, axis=-1, keepdims=True) + 1e-6) * q_scale
        o_ref[0, :, sl] = yh * jnp.where(j == 2, 1.0, rn)


def _dn_prepare(proj, conv_w8, n_lat, col_block):
    b, t_all, _ = proj.shape
    tt = SEQ_BLOCK
    bw = B_HEADS * B_HEAD_DIM
    r8 = tt // 8
    return pl.pallas_call(
        functools.partial(_dnprep_kernel, nt_lat=n_lat // tt, tt=tt),
        grid=(b, t_all // tt, 3),
        in_specs=[pl.BlockSpec((1, tt, bw), lambda bi, i, j: (bi, i, col_block + j)),
                  pl.BlockSpec((1, 8, bw), lambda bi, i, j: (bi, jnp.maximum(i * r8 - 1, 0), col_block + j)),
                  pl.BlockSpec((1, 8, bw), lambda bi, i, j: (bi, jnp.minimum((i + 1) * r8, t_all // 8 - 1),
                                                             col_block + j)),
                  pl.BlockSpec((8, bw), lambda bi, i, j: (0, j))],
        out_specs=pl.BlockSpec((1, tt, bw), lambda bi, i, j: (bi, i, j)),
        out_shape=jax.ShapeDtypeStruct((b, t_all, 3 * bw), f32),
        scratch_shapes=[pltpu.VMEM((tt + 16, bw), f32)],
        compiler_params=_cparams("parallel", "parallel", "parallel"),
        name="deltanet_prepare",
    )(proj, proj, proj, conv_w8)


def _dnscan_kernel(*refs, rev, add_prev):
    if add_prev:
        (q_ref, k_ref, v_ref, gc_ref, gt_ref, alr_ref, dtr_ref, alc_ref, dtc_ref, op_ref, o_ref,
         s_ref, u_scr, w_scr, qd_scr, kdt_scr, at_scr, vn_scr, egl_scr) = refs
    else:
        (q_ref, k_ref, v_ref, gc_ref, gt_ref, alr_ref, dtr_ref, alc_ref, dtc_ref, o_ref,
         s_ref, u_scr, w_scr, qd_scr, kdt_scr, at_scr, vn_scr, egl_scr) = refs
        op_ref = None
    tt = q_ref.shape[1]
    hd = B_HEAD_DIM
    d = 1 if rev else 0

    @pl.when(pl.program_id(1) == 0)
    def _():
        s_ref[...] = jnp.zeros(s_ref.shape, f32)

    same, incl, strict, row, col = _chunk_masks(tt, rev)
    tri = _mask01(incl)
    tri_t = _mask01(same & ((row >= col) if rev else (row <= col)))
    ones_bd = _mask01(same)
    eye = jnp.where(row == col, 1.0, 0.0)

    gates = gc_ref[0]
    g_all = -jnp.exp(alr_ref[...]) * _softplus(gates + dtr_ref[...])
    beta_all = _sigmoid(gates)
    gcum = _mm_lhs01(tri, g_all)
    gsum = _mm_lhs01(ones_bd, g_all)
    egl_scr[...] = jnp.exp(gsum)
    g_row = -jnp.exp(alc_ref[...]) * _softplus(gt_ref[0] + dtc_ref[...])
    gcum_r = _mm_rhs01(g_row, tri_t)

    ms, ps, rhss, m0s = [], [], [], []
    for h in range(B_HEADS):
        li = d * B_HEADS + h
        sl = slice(h * hd, (h + 1) * hd)
        qh, kh, vh = q_ref[0, :, sl], k_ref[0, :, sl], v_ref[0, :, sl]
        gc = gcum[:, li:li + 1]
        gs = gsum[:, li:li + 1]
        beta = beta_all[:, 2 * B_HEADS + li:2 * B_HEADS + li + 1]
        dec = jnp.where(incl, jnp.exp(jnp.where(incl, gc - gcum_r[li:li + 1, :], 0.0)), 0.0)
        kb = kh * beta
        kh16 = kh.astype(bf16)
        m = -jnp.where(strict, _mm_nt(kb.astype(bf16), kh16) * dec, 0.0)
        ms.append(m)
        m0s.append(_split2(m))
        ps.append(eye + m)
        egc = jnp.exp(gc)
        rhss.append(_split2(jnp.concatenate([vh * beta, kb * egc], axis=1)))
        at_scr[h] = jnp.where(incl, _mm_nt(qh.astype(bf16), kh16) * dec, 0.0)
        qd_scr[h] = qh * egc
        kdt_scr[h] = (kh * jnp.exp(gs - gc)).T
    for _ in range(5):
        for h in range(B_HEADS):
            m16 = ms[h].astype(bf16)
            ms[h] = _mm(m16, m16)
        for h in range(B_HEADS):
            ps[h] = ps[h] + _mm(ps[h].astype(bf16), ms[h].astype(bf16))
    for _ in range(DN_NEWTON_STEPS):
        xs = [_split2(ps[h]) for h in range(B_HEADS)]
        prods = [_mm(m0s[h][0], xs[h][0]) + _mm(m0s[h][0], xs[h][1]) + _mm(m0s[h][1], xs[h][0])
                 for h in range(B_HEADS)]
        resids = [((eye - ps[h]) + prods[h]).astype(bf16) for h in range(B_HEADS)]
        for h in range(B_HEADS):
            ps[h] = ps[h] + _mm(xs[h][0], resids[h])
    for h in range(B_HEADS):
        p_hi, p_lo = _split2(ps[h])
        r_hi, r_lo = rhss[h]
        sol = _mm(p_hi, r_hi) + _mm(p_hi, r_lo) + _mm(p_lo, r_hi)
        u_scr[h] = sol[:, :hd]
        vn_scr[h] = sol[:, :hd]
        w_scr[h] = sol[:, hd:]

    colk = lax.broadcasted_iota(jnp.int32, (hd, tt), 1) // CHUNK
    nchunk = tt // CHUNK
    for c in (range(nchunk - 1, -1, -1) if rev else range(nchunk)):
        rs = slice(c * CHUNK, (c + 1) * CHUNK)
        egl_row = egl_scr[c * CHUNK:c * CHUNK + 1, :]
        heads = range(B_HEADS)
        s_all = [s_ref[h] for h in heads]
        s16 = [s.astype(bf16) for s in s_all]
        for h in heads:
            vn_scr[h, rs, :] = u_scr[h, rs, :] - _mm(w_scr[h, rs, :].astype(bf16), s16[h])
        vn16 = [vn_scr[h].astype(bf16) for h in heads]
        for h in heads:
            sl = slice(h * hd, (h + 1) * hd)
            o = _mm(qd_scr[h, rs, :].astype(bf16), s16[h]) + _mm(at_scr[h, rs, :].astype(bf16), vn16[h])
            if op_ref is not None:
                o = o + op_ref[0, rs, sl]
            o_ref[0, rs, sl] = o
        for h in heads:
            li = d * B_HEADS + h
            kdt = jnp.where(colk == c, kdt_scr[h], 0.0).astype(bf16)
            s_ref[h] = s_all[h] * egl_row[:, li:li + 1] + _mm(kdt, vn16[h])


def _seq_block_index(i, nt_lat, rev):
    lat = (nt_lat - i) if rev else (i - 1)
    return jnp.where(i == 0, nt_lat, lat)


def _dn_scan(qkv, proj, gates_t, alog, dtb, o_prev, n_lat, gate_block, rev):
    b, t_all, _ = qkv.shape
    tt = SEQ_BLOCK
    nt_lat = n_lat // tt
    bw = B_HEADS * B_HEAD_DIM
    hd = B_HEAD_DIM
    blk = lambda bi, i: _seq_block_index(i, nt_lat, rev)
    pad = jnp.zeros((LANES - 2 * B_HEADS,), f32)
    alr = jnp.concatenate([alog.reshape(-1), pad]).reshape(1, LANES)
    dtr = jnp.concatenate([dtb.reshape(-1), pad]).reshape(1, LANES)
    alc = jnp.concatenate([alog.reshape(-1), pad[:2 * B_HEADS]]).reshape(4 * B_HEADS, 1)
    dtc = jnp.concatenate([dtb.reshape(-1), pad[:2 * B_HEADS]]).reshape(4 * B_HEADS, 1)
    small = lambda shape: pl.BlockSpec(shape, lambda bi, i: (0, 0))
    in_specs = [pl.BlockSpec((1, tt, bw), lambda bi, i: (bi, blk(bi, i), 0)),
                pl.BlockSpec((1, tt, bw), lambda bi, i: (bi, blk(bi, i), 1)),
                pl.BlockSpec((1, tt, bw), lambda bi, i: (bi, blk(bi, i), 2)),
                pl.BlockSpec((1, tt, LANES), lambda bi, i: (bi, blk(bi, i), gate_block)),
                pl.BlockSpec((1, 4 * B_HEADS, tt), lambda bi, i: (bi, 0, blk(bi, i))),
                small((1, LANES)), small((1, LANES)), small((4 * B_HEADS, 1)), small((4 * B_HEADS, 1))]
    args = [qkv, qkv, qkv, proj, gates_t, alr, dtr, alc, dtc]
    if o_prev is not None:
        in_specs.append(pl.BlockSpec((1, tt, bw), lambda bi, i: (bi, blk(bi, i), 0)))
        args.append(o_prev)
    head = lambda *s: pltpu.VMEM((B_HEADS,) + s, f32)
    return pl.pallas_call(
        functools.partial(_dnscan_kernel, rev=rev, add_prev=o_prev is not None),
        grid=(b, nt_lat + 1),
        in_specs=in_specs,
        out_specs=pl.BlockSpec((1, tt, bw), lambda bi, i: (bi, blk(bi, i), 0)),
        out_shape=jax.ShapeDtypeStruct((b, t_all, bw), f32),
        scratch_shapes=[head(hd, hd), head(tt, hd), head(tt, hd), head(tt, hd), head(hd, tt), head(tt, tt),
                        head(tt, hd), pltpu.VMEM((tt, LANES), f32)],
        compiler_params=_cparams("parallel", "arbitrary"),
        name="deltanet_scan_bwd" if rev else "deltanet_scan_fwd",
    )(*args)


def _glascan_kernel(*refs, rev, add_prev):
    if add_prev:
        q_ref, k_ref, v_ref, r_ref, w2_ref, gb_ref, op_ref, o_ref, s_ref = refs
    else:
        q_ref, k_ref, v_ref, r_ref, w2_ref, gb_ref, o_ref, s_ref = refs
        op_ref = None
    tt = q_ref.shape[1]
    dk = q_ref.shape[2] // C_HEADS
    dv = v_ref.shape[2] // C_HEADS
    heads = range(C_HEADS)

    @pl.when(pl.program_id(1) == 0)
    def _():
        s_ref[...] = jnp.zeros(s_ref.shape, f32)

    same, incl, _, _, _ = _chunk_masks(tt, rev)
    tri = _mask01(incl)
    ones_bd = _mask01(same)
    z = _mm_x3(r_ref[0], w2_ref[...]) + gb_ref[...]
    gk = (jnp.minimum(z, 0.0) - jnp.log(1.0 + jnp.exp(-jnp.abs(z)))) * (1.0 / GATE_NORMALIZER)
    bcum = _mm_lhs01(tri, gk)
    bsum = _mm_lhs01(ones_bd, gk)
    nchunk = tt // CHUNK
    mid = (CHUNK - 1 - CHUNK // 2) if rev else CHUNK // 2
    bmid = jnp.concatenate(
        [jnp.broadcast_to(bcum[c * CHUNK + mid:c * CHUNK + mid + 1, :], (CHUNK, bcum.shape[1]))
         for c in range(nchunk)], axis=0)
    q = q_ref[0] * (dk ** -0.5)
    k = k_ref[0]
    v16 = v_ref[0].astype(bf16)
    qe = (q * jnp.exp(bcum - bmid)).astype(bf16)
    ke = (k * jnp.exp(bmid - bcum)).astype(bf16)
    q_dec = (q * jnp.exp(bcum)).astype(bf16)
    kd = k * jnp.exp(bsum - bcum)
    dec = jnp.exp(bsum)
    ks = [slice(h * dk, (h + 1) * dk) for h in heads]
    vs = [slice(h * dv, (h + 1) * dv) for h in heads]
    attn = [jnp.where(incl, _mm_nt(qe[:, ks[h]], ke[:, ks[h]]), 0.0).astype(bf16) for h in heads]
    o_intra = [_mm(attn[h], v16[:, vs[h]]) for h in heads]
    kd_t = [kd[:, ks[h]].T for h in heads]
    dec_t = [dec[:, ks[h]].T for h in heads]
    colk = lax.broadcasted_iota(jnp.int32, (dk, tt), 1) // CHUNK
    for c in (range(nchunk - 1, -1, -1) if rev else range(nchunk)):
        rs = slice(c * CHUNK, (c + 1) * CHUNK)
        s_all = [s_ref[h] for h in heads]
        for h in heads:
            o = o_intra[h][rs, :] + _mm(q_dec[rs, ks[h]], s_all[h].astype(bf16))
            if op_ref is not None:
                o = o + op_ref[0, rs, vs[h]]
            o_ref[0, rs, vs[h]] = o
        for h in heads:
            kdt = jnp.where(colk == c, kd_t[h], 0.0).astype(bf16)
            s_ref[h] = s_all[h] * dec_t[h][:, c * CHUNK:c * CHUNK + 1] + _mm(kdt, v16[:, vs[h]])


def _gla_scan(proj, w2pad, gate_b, o_prev, n_lat, rev):
    b, t_all, _ = proj.shape
    tt = SEQ_BLOCK
    nt_lat = n_lat // tt
    dk_all = w2pad.shape[1]
    dk = dk_all // C_HEADS
    dv_all = 2 * dk_all
    blk = lambda i: _seq_block_index(i, nt_lat, rev)
    in_specs = [pl.BlockSpec((1, tt, dk_all), lambda bi, i: (bi, blk(i), 0)),
                pl.BlockSpec((1, tt, dk_all), lambda bi, i: (bi, blk(i), 1)),
                pl.BlockSpec((1, tt, dv_all), lambda bi, i: (bi, blk(i), 1)),
                pl.BlockSpec((1, tt, LANES), lambda bi, i: (bi, blk(i), (2 * dk_all + 2 * dv_all) // LANES)),
                pl.BlockSpec((LANES, dk_all), lambda bi, i: (0, 0)),
                pl.BlockSpec((1, dk_all), lambda bi, i: (0, 0))]
    args = [proj, proj, proj, proj, w2pad, gate_b]
    if o_prev is not None:
        in_specs.append(pl.BlockSpec((1, tt, dv_all), lambda bi, i: (bi, blk(i), 0)))
        args.append(o_prev)
    return pl.pallas_call(
        functools.partial(_glascan_kernel, rev=rev, add_prev=o_prev is not None),
        grid=(b, nt_lat + 1),
        in_specs=in_specs,
        out_specs=pl.BlockSpec((1, tt, dv_all), lambda bi, i: (bi, blk(i), 0)),
        out_shape=jax.ShapeDtypeStruct((b, t_all, dv_all), f32),
        scratch_shapes=[pltpu.VMEM((C_HEADS, dk, 2 * dk), f32)],
        compiler_params=_cparams("parallel", "arbitrary"),
        name="gla_scan_bwd" if rev else "gla_scan_fwd",
    )(*args)


def _out_kernel(*refs, group, has_a):
    if has_a:
        a_ref, o_ref, z_ref, gn_ref, w_ref, x_ref, mod_ref, g2_ref, wr_ref, br_ref, xo_ref, ho_ref, lo_ref = refs
    else:
        o_ref, z_ref, gn_ref, w_ref, x_ref, mod_ref, g2_ref, wr_ref, br_ref, xo_ref, ho_ref, lo_ref = refs
    width = o_ref.shape[2]
    z = z_ref[0]
    gate = z * _sigmoid(z)
    parts = []
    for g0 in range(0, width, group):
        og = o_ref[0, :, g0:g0 + group]
        parts.append(og * lax.rsqrt(jnp.mean(og * og, axis=-1, keepdims=True) + NORM_EPS))
    on = (jnp.concatenate(parts, axis=1) * gn_ref[...] * gate).astype(bf16)
    if has_a:
        na = a_ref.shape[2]
        y = _mm(a_ref[0], w_ref[0:na, :]) + _mm(on, w_ref[na:na + width, :])
    else:
        y = _mm(on, w_ref[...])
    xn = x_ref[0] + mod_ref[0, 2:3, :] * y
    xo_ref[0] = xn
    hn = xn * lax.rsqrt(jnp.mean(xn * xn, axis=-1, keepdims=True) + NORM_EPS) * g2_ref[...]
    h2 = hn * (1.0 + mod_ref[0, 4:5, :]) + mod_ref[0, 3:4, :]
    ho_ref[0] = h2
    lo_ref[0] = _mm_x3(h2, wr_ref[...]) + br_ref[...]


def _mixer_output(a, o, proj, z_block, gnorm, w_out, x_all, mod_l, g2, wr, br, n_lat, group):
    b, t_all, d = x_all.shape
    tm = SEQ_BLOCK
    nt_lat = n_lat // tm
    width = o.shape[2]
    row = lambda w: pl.BlockSpec((1, tm, w), lambda bi, i: (bi, i, 0))
    const = lambda shape: pl.BlockSpec(shape, lambda bi, i: (0,) * len(shape))
    in_specs, args = [], []
    if a is not None:
        in_specs.append(row(a.shape[2]))
        args.append(a)
    in_specs += [row(width),
                 pl.BlockSpec((1, tm, width), lambda bi, i: (bi, i, z_block)),
                 const((1, width)), const(w_out.shape), row(d),
                 pl.BlockSpec((1, 6, d), lambda bi, i: (jnp.where(i >= nt_lat, 2, bi), 0, 0)),
                 const((1, d)), const(wr.shape), const((1, LANES))]
    args += [o, proj, gnorm, w_out, x_all, mod_l, g2.reshape(1, d), wr, br]
    return pl.pallas_call(
        functools.partial(_out_kernel, group=group, has_a=a is not None),
        grid=(b, t_all // tm),
        in_specs=in_specs,
        out_specs=[row(d), row(d), row(LANES)],
        out_shape=[jax.ShapeDtypeStruct((b, t_all, d), f32), jax.ShapeDtypeStruct((b, t_all, d), f32),
                   jax.ShapeDtypeStruct((b, t_all, LANES), f32)],
        compiler_params=_cparams("parallel", "parallel"),
        name="mixer_output",
    )(*args)


def _moe_kernel(be_ref, code_ref, nused_ref, h_hbm, wg_ref, wl_ref, bg_ref, bl_ref, wd_ref, bd_ref, sw_ref, y_hbm,
                xbuf, x16, acc, gsem, ssem, *, tm, nf, n_tok):
    m = pl.program_id(0)
    f = pl.program_id(1)
    nu = nused_ref[0]
    active = m < nu
    cur = m % 2
    oth = 1 - cur
    rows = tm // nf

    def gather_start(blk, buf, r, priority=0):
        tok = code_ref[blk * tm + r] & 0xFFFF
        pltpu.make_async_copy(h_hbm.at[pl.ds(tok, 1)], xbuf.at[buf, pl.ds(r, 1)], gsem.at[buf]).start(
            priority=priority)

    def gather_wait(buf):
        pltpu.make_async_copy(h_hbm.at[pl.ds(0, tm)], xbuf.at[buf], gsem.at[buf]).wait()

    def scatter_start(blk, buf, r, real, priority=0):
        code = code_ref[blk * tm + r]
        dst = ((code >> 16) & 7) * n_tok + (code & 0xFFFF)
        dst = jnp.where(real, dst, TOP_K * n_tok + r)
        pltpu.make_async_copy(acc.at[buf, pl.ds(r, 1)], y_hbm.at[pl.ds(dst, 1)], ssem).start(priority=priority)

    def scatter_wait():
        pltpu.make_async_copy(acc.at[0], y_hbm.at[pl.ds(0, tm)], ssem).wait()

    @pl.when(active & (f == 0))
    def _():
        @pl.when(m == 0)
        def _():
            def issue(r, carry):
                gather_start(0, 0, r)
                return carry
            lax.fori_loop(0, tm, issue, 0)
            acc[1] = jnp.zeros(acc.shape[1:], f32)

        gather_wait(cur)

        @pl.when(m > 0)
        def _():
            scatter_wait()

        x16[...] = xbuf[cur].astype(bf16)

    nxt = jnp.minimum(m + 1, pl.num_programs(0) - 1)
    prv = jnp.maximum(m - 1, 0)
    for fs in range(nf):
        @pl.when(active & (f == fs))
        def _(fs=fs):
            for r in range(fs * rows, (fs + 1) * rows):
                gather_start(nxt, oth, r, priority=r % 2)
                scatter_start(prv, oth, r, m > 0, priority=(r + 1) % 2)
            x = x16[...]
            glu = jnp.minimum(_mm(x, wg_ref[0]) + bg_ref[0], SWIGLU_LIMIT)
            lin = jnp.clip(_mm(x, wl_ref[0]) + bl_ref[0], -SWIGLU_LIMIT, SWIGLU_LIMIT)
            act = glu * _sigmoid(SWIGLU_ALPHA * glu) * (lin + 1.0)
            y = _mm(act.astype(bf16), wd_ref[0])
            if fs > 0:
                y = acc[cur] + y
            if fs == nf - 1:
                y = (y + bd_ref[0]) * sw_ref[...]
            acc[cur] = y

    @pl.when((m == nu - 1) & (f == nf - 1))
    def _():
        scatter_wait()

        def issue(r, carry):
            scatter_start(m, cur, r, True)
            return carry
        lax.fori_loop(0, tm, issue, 0)
        scatter_wait()
        gather_wait(oth)


def _moe_experts(h2, block_expert, slot_code, n_used, slot_w, w_gu, b_gu, w_down, b_down, layer):
    n_tok, d = h2.shape
    _, n_exp, _, ff2 = w_gu.shape
    ff = ff2 // 2
    tm, tf = MOE_ROWS, min(MOE_FF_TILE, ff)
    nf = ff // tf
    n_blocks = block_expert.shape[0]
    assert n_tok < (1 << 16) and tm % nf == 0

    def live_expert(m, be, nu):
        return be[jnp.minimum(m, nu[0] - 1)]

    def live_tile(m, f, nu):
        return jnp.where(m < nu[0], f, nf - 1)

    def wmap(off):
        return lambda m, f, be, code, nu: (layer, live_expert(m, be, nu), 0, off + live_tile(m, f, nu))

    grid_spec = pltpu.PrefetchScalarGridSpec(
        num_scalar_prefetch=3,
        grid=(n_blocks, nf),
        in_specs=[pl.BlockSpec(memory_space=pl.ANY),
                  pl.BlockSpec((None, 1, d, tf), wmap(0)),
                  pl.BlockSpec((None, 1, d, tf), wmap(nf)),
                  pl.BlockSpec((None, 1, 1, tf), wmap(0)),
                  pl.BlockSpec((None, 1, 1, tf), wmap(nf)),
                  pl.BlockSpec((None, 1, tf, d),
                               lambda m, f, be, code, nu: (layer, live_expert(m, be, nu), live_tile(m, f, nu), 0)),
                  pl.BlockSpec((None, 1, 1, d), lambda m, f, be, code, nu: (layer, live_expert(m, be, nu), 0, 0)),
                  pl.BlockSpec((tm, 1), lambda m, f, be, code, nu: (jnp.minimum(m, nu[0] - 1), 0))],
        out_specs=pl.BlockSpec(memory_space=pl.ANY),
        scratch_shapes=[pltpu.VMEM((2, tm, d), f32), pltpu.VMEM((tm, d), bf16), pltpu.VMEM((2, tm, d), f32),
                        pltpu.SemaphoreType.DMA((2,)), pltpu.SemaphoreType.DMA(())])
    return pl.pallas_call(
        functools.partial(_moe_kernel, tm=tm, nf=nf, n_tok=n_tok),
        grid_spec=grid_spec,
        out_shape=jax.ShapeDtypeStruct((TOP_K * n_tok + tm, d), f32),
        compiler_params=_cparams("arbitrary", "arbitrary"),
        name="moe_experts",
    )(block_expert, slot_code, n_used, h2, w_gu, w_gu, b_gu, b_gu, w_down, b_down, slot_w)


def _combine_kernel(y0_ref, y1_ref, y2_ref, y3_ref, x_ref, mod_ref, o_ref):
    f = (y0_ref[...] + y1_ref[...]) + (y2_ref[...] + y3_ref[...])
    o_ref[0] = x_ref[0] + mod_ref[0, 5:6, :] * f


def _moe_combine(y4, x_all, mod_l, n_lat, n_rows_out):
    b, t_all, d = x_all.shape
    tb = SEQ_BLOCK
    nt_lat = n_lat // tb
    per_k = b * t_all // tb
    per_b = t_all // tb
    yspec = lambda k: pl.BlockSpec((tb, d), lambda bi, i: (k * per_k + bi * per_b + i, 0))
    return pl.pallas_call(
        _combine_kernel,
        grid=(b, n_rows_out // tb),
        in_specs=[yspec(0), yspec(1), yspec(2), yspec(3),
                  pl.BlockSpec((1, tb, d), lambda bi, i: (bi, i, 0)),
                  pl.BlockSpec((1, 6, d), lambda bi, i: (jnp.where(i >= nt_lat, 2, bi), 0, 0))],
        out_specs=pl.BlockSpec((1, tb, d), lambda bi, i: (bi, i, 0)),
        out_shape=jax.ShapeDtypeStruct((b, n_rows_out, d), f32),
        compiler_params=_cparams("parallel", "parallel"),
        name="moe_combine",
    )(y4, y4, y4, y4, x_all, mod_l)


def _route(logits, n_exp):
    n = logits.shape[0]
    tm = MOE_ROWS
    i32 = jnp.int32
    top_val, top_idx = lax.top_k(logits, TOP_K)
    gate = jax.nn.softmax(top_val, axis=-1)
    e_flat = top_idx.reshape(-1).astype(i32)
    order = jnp.argsort(e_flat).astype(i32)
    counts = jnp.sum((e_flat[None, :] == jnp.arange(n_exp, dtype=i32)[:, None]).astype(i32), axis=1)
    padded = (counts + tm - 1) // tm * tm
    start = jnp.cumsum(counts) - counts
    pad_end = jnp.cumsum(padded)
    pad_start = pad_end - padded
    n_blocks = -(-(n * TOP_K) // tm) + n_exp
    blk_start = jnp.arange(n_blocks, dtype=i32) * tm
    block_expert = jnp.minimum(jnp.sum((pad_end[None, :] <= blk_start[:, None]).astype(i32), axis=1), n_exp - 1)
    n_used = (pad_end[-1] // tm).astype(i32).reshape(1)
    row = jnp.arange(tm, dtype=i32)[None, :]
    off = blk_start[:, None] + row - pad_start[block_expert][:, None]
    valid = (off < counts[block_expert][:, None]) & (blk_start[:, None] < pad_end[-1])
    j = jnp.clip(start[block_expert][:, None] + off, 0, n * TOP_K - 1)
    item = order[j]
    code = jnp.where(valid, (item // TOP_K) | ((item % TOP_K) << 16), row | (TOP_K << 16)).astype(i32)
    slot_w = jnp.where(valid, gate.reshape(-1)[item], 0.0).astype(f32)
    return block_expert.astype(i32), code.reshape(-1), n_used, slot_w.reshape(-1, 1)


def _rope_perm():
    q = A_HEAD_DIM // 4
    one = np.concatenate([np.arange(0, q), np.arange(2 * q, 3 * q), np.arange(q, 2 * q), np.arange(3 * q, 4 * q)])
    return one


def _rope_tables(n_lat, n_ctx):
    rows = n_lat // GRID_W
    row = jnp.repeat(jnp.arange(rows, dtype=f32), GRID_W)
    col = jnp.tile(jnp.arange(GRID_W, dtype=f32), rows)
    n_freq = A_HEAD_DIM // 4
    inv = ROPE_BASE ** (-jnp.arange(n_freq, dtype=f32) / n_freq)
    ang = jnp.concatenate([row[:, None] * inv, col[:, None] * inv], axis=1)
    cos, sin = jnp.cos(ang), jnp.sin(ang)
    cos_t = jnp.tile(cos, (1, 4))
    sin_t = jnp.tile(jnp.concatenate([-sin, sin], axis=1), (1, 2))
    cos_t = jnp.concatenate([cos_t, jnp.ones((n_ctx, LANES), f32)], axis=0)
    sin_t = jnp.concatenate([sin_t, jnp.zeros((n_ctx, LANES), f32)], axis=0)
    return cos_t, sin_t


def _pad_cols(w, n):
    return jnp.pad(w, ((0, 0), (0, n - w.shape[1])))


def _moe_layer(h2, logits, x_all, mod_l, w_gu16, b_gu4, w_down16, b_down4, layer, n_lat, n_rows_out):
    b, t_all, d = x_all.shape
    n_exp = w_gu16.shape[1]
    block_expert, slot_code, n_used, slot_w = _route(logits.reshape(b * t_all, LANES)[:, :n_exp], n_exp)
    y4 = _moe_experts(h2.reshape(b * t_all, d), block_expert, slot_code, n_used, slot_w,
                      w_gu16, b_gu4, w_down16, b_down4, layer)
    return _moe_combine(y4, x_all, mod_l, n_lat, n_rows_out)


def kernel(x, c, ctx, c_ctx, w_mod, b_mod, norm_g, e_w_in, e_w_out, e_q_gain, e_k_gain, e_sinks, e_conv_w, e_a_log, e_dt_bias, e_dn_norm, o_w_in, o_gate_w2, o_gate_b, o_gla_norm, o_w_out, w_router, b_router, w_gu, b_gu, w_down, b_down):
    b, s, d = x.shape
    n_ctx = ctx.shape[1]
    depth = w_mod.shape[0]
    n_exp = w_router.shape[2]
    assert n_ctx == SEQ_BLOCK and s % SEQ_BLOCK == 0 and b == 2
    x_all = jnp.concatenate([x, ctx], axis=1)
    t_all = s + n_ctx

    c8 = jnp.concatenate([c, c_ctx[None, :], jnp.zeros((8 - b - 1, d), f32)], axis=0)
    mod = _mod_vectors(c8, w_mod, b_mod).reshape(depth, 8, 6, d)
    wr = jnp.pad(w_router, ((0, 0), (0, 0), (0, LANES - n_exp)))
    br = jnp.pad(b_router, ((0, 0), (0, LANES - n_exp))).reshape(depth, 1, LANES)
    cos_t, sin_t = _rope_tables(s, n_ctx)
    perm = _rope_perm()
    w_gu16, w_down16 = w_gu.astype(bf16), w_down.astype(bf16)
    b_gu4 = b_gu.reshape(depth, n_exp, 1, b_gu.shape[2])
    b_down4 = b_down.reshape(depth, n_exp, 1, d)
    aq = A_HEADS * A_HEAD_DIM
    akv = A_KV_HEADS * A_HEAD_DIM
    bw = B_HEADS * B_HEAD_DIM

    for layer in range(depth):
        i = layer // 2
        mod_l = mod[layer]
        last = layer == depth - 1
        if layer % 2 == 0:
            w = e_w_in[i]
            qcols = (np.arange(A_HEADS)[:, None] * A_HEAD_DIM + perm[None, :]).reshape(-1)
            kcols = aq + (np.arange(A_KV_HEADS)[:, None] * A_HEAD_DIM + perm[None, :]).reshape(-1)
            o_dn = aq + 2 * akv
            o_z = o_dn + 3 * bw
            o_g = o_z + bw
            w_cat = jnp.concatenate([w[:, qcols], w[:, o_z:o_g], w[:, o_dn:o_z], w[:, kcols],
                                     w[:, aq + akv:aq + 2 * akv], w[:, o_g:]], axis=1)
            n_pad = 5760
            w16 = _pad_cols(w_cat, n_pad).astype(bf16)
            proj = _input_projection(x_all, norm_g[layer, 0], mod_l, w16, s, 1152)
            gain = jnp.concatenate([jnp.tile(e_q_gain[i][perm], A_HEADS) * (A_HEAD_DIM ** -0.5),
                                    jnp.tile(e_k_gain[i][perm], A_KV_HEADS)]).reshape(1, aq + akv)
            q16, kt16, vt16 = _qk_prepare(proj, cos_t, sin_t, gain, (2 * bw + 3 * bw) // 512)
            a_all = _attention(e_sinks[i], q16, kt16, vt16, s)
            conv8 = jnp.pad(e_conv_w[i], ((0, 8 - CONV_W), (0, 0)))
            qkv = _dn_prepare(proj, conv8, s, 2)
            gate_block = (n_pad - LANES) // LANES
            gates_t = jnp.swapaxes(proj[:, :, n_pad - LANES:n_pad - LANES + 4 * B_HEADS], 1, 2)
            o_f = _dn_scan(qkv, proj, gates_t, e_a_log[i], e_dt_bias[i], None, s, gate_block, rev=False)
            o_mix = _dn_scan(qkv, proj, gates_t, e_a_log[i], e_dt_bias[i], o_f, s, gate_block, rev=True)
            gnorm = jnp.tile(e_dn_norm[i], B_HEADS).reshape(1, bw)
            x_all, h2, logits = _mixer_output(a_all, o_mix, proj, 1, gnorm, e_w_out[i].astype(bf16), x_all, mod_l,
                                              norm_g[layer, 1], wr[layer], br[layer], s, B_HEAD_DIM)
        else:
            w = o_w_in[i]
            n_pad = 6272
            proj = _input_projection(x_all, norm_g[layer, 0], mod_l, _pad_cols(w, n_pad).astype(bf16), s, 896)
            dk_all = o_gate_w2.shape[3]
            o_mix = None
            for dirn in range(2):
                w2pad = jnp.zeros((LANES, dk_all), f32).at[dirn * GATE_RANK:(dirn + 1) * GATE_RANK].set(
                    o_gate_w2[i, dirn])
                o_mix = _gla_scan(proj, w2pad, o_gate_b[i, dirn].reshape(1, dk_all), o_mix, s, rev=dirn == 1)
            dv = o_gla_norm.shape[1]
            gnorm = jnp.tile(o_gla_norm[i], C_HEADS).reshape(1, C_HEADS * dv)
            x_all, h2, logits = _mixer_output(None, o_mix, proj, 2, gnorm, o_w_out[i].astype(bf16), x_all, mod_l,
                                              norm_g[layer, 1], wr[layer], br[layer], s, dv)
        x_all = _moe_layer(h2, logits, x_all, mod_l, w_gu16, b_gu4, w_down16, b_down4, layer, s,
                           s if last else t_all)
    return x_all
```
